```python
import math
import jax
import jax.numpy as jnp
from jax import lax
import numpy as np

D_MODEL = 4096
BATCH = 4
SEQ = 2048
DEPTH = 2
DEC_BATCH = 16
DEC_SEQ = 64
PAST_LEN = 4096

CHUNK = 64
N_EVEN = (DEPTH + 1) // 2
N_ODD = DEPTH // 2
H_A = 4
DV_A = D_MODEL // 8
DK_A = DV_A // 2
GATE_RANK = 16
GATE_NORM = 16.0
H_B = 16
HD_B = D_MODEL // 32
BAND_PAST_CHUNKS = 8
BAND_PAST = BAND_PAST_CHUNKS * CHUNK
BAND = BAND_PAST + CHUNK
REL_CLIP = 128
H_C = 16
HD_C = D_MODEL // 32
ROT_DIM = HD_C // 4
ROPE_THETA = 500000.0
Q_BLOCK = 128
D_FF = ((8 * D_MODEL // 3 + 255) // 256) * 256
CONV_W = 3
EPS = 1e-6

W_A = H_A * DV_A
W_B = H_B * HD_B
D_MIX_EVEN = W_A + W_B
W_C = H_C * 2 * HD_C
D_IN_EVEN = 2 * H_A * DK_A + 2 * W_A + GATE_RANK + 3 * W_B
D_IN_ODD = 3 * W_C

kernel_name = 'hybrid_stream_encoder_step'


def rms_norm(x, w):
    xf = x.astype(jnp.float32)
    y = xf * lax.rsqrt(jnp.mean(xf * xf, axis=-1, keepdims=True) + EPS)
    return (y * w.astype(jnp.float32)).astype(x.dtype)


def partial_rotary(x, pos):
    half = ROT_DIM // 2
    inv_freq = ROPE_THETA ** (-jnp.arange(half, dtype=jnp.float32) / half)
    ang = pos.astype(jnp.float32)[:, None] * inv_freq[None, :]
    cos = jnp.cos(ang)[None, :, None, :]
    sin = jnp.sin(ang)[None, :, None, :]
    xf = x.astype(jnp.float32)
    x1, x2 = xf[..., :half], xf[..., half:ROT_DIM]
    rot = jnp.concatenate([x1 * cos - x2 * sin, x2 * cos + x1 * sin], axis=-1).astype(x.dtype)
    return jnp.concatenate([rot, x[..., ROT_DIM:]], axis=-1)


def gla_recurrence(q, k, v, log_a, s0):
    B, T, H, DK = q.shape
    DV = v.shape[-1]
    L = min(CHUNK, T)
    nb = T // L

    def blocks(t):
        return jnp.moveaxis(t.astype(jnp.float32).reshape(B, nb, L, *t.shape[2:]), 1, 0)

    tri = jnp.tril(jnp.ones((L, L), dtype=bool))[None, :, :, None, None]

    def step(s, inp):
        qb, kb, vb, ab = inp
        cum = jnp.cumsum(ab, axis=1)
        o_inter = jnp.einsum('blhk,bhkv->blhv', qb * jnp.exp(cum), s)
        decay = jnp.exp(jnp.where(tri, cum[:, :, None] - cum[:, None, :], -jnp.inf))
        attn = jnp.einsum('bijhk,bjhk->bhij', qb[:, :, None] * decay, kb)
        o_intra = jnp.einsum('bhij,bjhv->bihv', attn, vb)
        last = cum[:, -1]
        s_new = jnp.exp(last)[..., None] * s + jnp.einsum('bjhk,bjhv->bhkv', kb * jnp.exp(last[:, None] - cum), vb)
        return s_new, o_inter + o_intra

    s_fin, o = lax.scan(step, s0.astype(jnp.float32), (blocks(q), blocks(k), blocks(v), blocks(log_a)))
    return jnp.moveaxis(o, 0, 1).reshape(B, T, H, DV), s_fin


def rel_bias(table, dist):
    return table[:, jnp.clip(dist, -REL_CLIP, REL_CLIP) + REL_CLIP].astype(jnp.float32)


def band_attention_prompt(q, k, v, table):
    B, T, H, D = q.shape
    nc = T // CHUNK
    kp = jnp.pad(k, ((0, 0), (BAND_PAST, 0), (0, 0), (0, 0)))
    vp = jnp.pad(v, ((0, 0), (BAND_PAST, 0), (0, 0), (0, 0)))
    idx = jnp.arange(nc)[:, None] * CHUNK + jnp.arange(BAND)[None, :]
    kb = kp[:, idx]
    vb = vp[:, idx]
    valid = idx >= BAND_PAST
    qc = q.reshape(B, nc, CHUNK, H, D)
    s = jnp.einsum('bcqhd,bckhd->bchqk', qc, kb).astype(jnp.float32) * (D ** -0.5)
    dist = BAND_PAST + jnp.arange(CHUNK)[:, None] - jnp.arange(BAND)[None, :]
    s = s + rel_bias(table, dist)[None, None]
    s = jnp.where(valid[None, :, None, None, :], s, -jnp.inf)
    p = jax.nn.softmax(s, axis=-1)
    o = jnp.einsum('bchqk,bckhd->bcqhd', p, vb)
    return o.reshape(B, T, H, D)


def band_attention_step(q, k, v, k_hist, v_hist, table):
    B, T, H, D = q.shape
    keep = k_hist.shape[1]
    kk = jnp.concatenate([k_hist.astype(k.dtype), k], axis=1)
    vv = jnp.concatenate([v_hist.astype(v.dtype), v], axis=1)
    q_pos = PAST_LEN + jnp.arange(T)
    k_pos = jnp.concatenate([PAST_LEN - keep + jnp.arange(keep), q_pos])
    s = jnp.einsum('bqhd,bkhd->bhqk', q, kk).astype(jnp.float32) * (D ** -0.5)
    s = s + rel_bias(table, q_pos[:, None] - k_pos[None, :])[None]
    p = jax.nn.softmax(s, axis=-1)
    return jnp.einsum('bhqk,bkhd->bqhd', p, vv)


def diff_weights(q1, q2, k1, k2, lam, mask):
    scale = HD_C ** -0.5
    s1 = jnp.einsum('bqhd,bkhd->bhqk', q1, k1).astype(jnp.float32) * scale
    s2 = jnp.einsum('bqhd,bkhd->bhqk', q2, k2).astype(jnp.float32) * scale
    if mask is not None:
        s1 = jnp.where(mask, s1, -jnp.inf)
        s2 = jnp.where(mask, s2, -jnp.inf)
    return jax.nn.softmax(s1, axis=-1) - lam * jax.nn.softmax(s2, axis=-1)


def diff_attention_prompt(q1, q2, k1, k2, v, lam):
    B, T, H, D = q1.shape
    nq = T // Q_BLOCK
    k_chunk = jnp.arange(T) // CHUNK

    def blk(inp):
        q1b, q2b, q_pos = inp
        mask = ((q_pos // CHUNK)[:, None] >= k_chunk[None, :])[None, None]
        p = diff_weights(q1b, q2b, k1, k2, lam, mask)
        return jnp.einsum('bhqk,bkhv->bqhv', p, v)

    def split(t):
        return jnp.moveaxis(t.reshape(B, nq, Q_BLOCK, H, D), 1, 0)

    o = lax.map(blk, (split(q1), split(q2), jnp.arange(T).reshape(nq, Q_BLOCK)))
    return jnp.moveaxis(o, 0, 1).reshape(B, T, H, v.shape[-1])


def even_mixer(h, hist, w_in, w_a_up, b_a, gla_w, table, w_out):
    B, T, _ = h.shape
    sizes = [H_A * DK_A, H_A * DK_A, W_A, W_A, GATE_RANK, W_B, W_B, W_B]
    cuts = [int(c) for c in np.cumsum(sizes)[:-1]]
    q_a, k_a, v_a, g_a, r_a, q_b, k_b, v_b = jnp.split(h @ w_in, cuts, axis=-1)
    log_a = jax.nn.log_sigmoid((r_a @ w_a_up + b_a).astype(jnp.float32)) / GATE_NORM
    s0 = jnp.zeros((B, H_A, DK_A, DV_A), jnp.float32) if hist is None else hist[0]
    o_a, s_fin = gla_recurrence(q_a.reshape(B, T, H_A, DK_A) * (DK_A ** -0.5), k_a.reshape(B, T, H_A, DK_A),
                                v_a.reshape(B, T, H_A, DV_A), log_a.reshape(B, T, H_A, DK_A), s0)
    o_a = rms_norm(o_a, gla_w) * jax.nn.silu(g_a.reshape(B, T, H_A, DV_A).astype(jnp.float32))
    q_b = q_b.reshape(B, T, H_B, HD_B)
    k_b = k_b.reshape(B, T, H_B, HD_B)
    v_b = v_b.reshape(B, T, H_B, HD_B)
    if hist is None:
        o_b = band_attention_prompt(q_b, k_b, v_b, table)
        keep = min(BAND_PAST, T)
        k_rows, v_rows = k_b[:, T - keep:], v_b[:, T - keep:]
    else:
        o_b = band_attention_step(q_b, k_b, v_b, hist[1], hist[2], table)
        k_rows, v_rows = k_b, v_b
    o = jnp.concatenate([o_a.reshape(B, T, W_A).astype(h.dtype), o_b.reshape(B, T, W_B).astype(h.dtype)], axis=-1)
    return o @ w_out, s_fin.astype(h.dtype), k_rows, v_rows


def odd_mixer(h, pos, hist, layer, w_in, lq1, lk1, lq2, lk2, norm_w, w_out):
    B, T, _ = h.shape
    q, k, v = jnp.split(h @ w_in, 3, axis=-1)
    q = partial_rotary(q.reshape(B, T, 2 * H_C, HD_C), pos).reshape(B, T, H_C, 2 * HD_C)
    k = partial_rotary(k.reshape(B, T, 2 * H_C, HD_C), pos).reshape(B, T, H_C, 2 * HD_C)
    v = v.reshape(B, T, H_C, 2 * HD_C)
    lam_init = 0.8 - 0.6 * math.exp(-0.3 * layer)
    lam = (jnp.exp(jnp.sum(lq1.astype(jnp.float32) * lk1.astype(jnp.float32)))
           - jnp.exp(jnp.sum(lq2.astype(jnp.float32) * lk2.astype(jnp.float32))) + lam_init)
    q1, q2 = q[..., :HD_C], q[..., HD_C:]
    if hist is None:
        o = diff_attention_prompt(q1, q2, k[..., :HD_C], k[..., HD_C:], v, lam)
    else:
        kk = jnp.concatenate([hist[0].astype(k.dtype), k], axis=1)
        vv = jnp.concatenate([hist[1].astype(v.dtype), v], axis=1)
        p = diff_weights(q1, q2, kk[..., :HD_C], kk[..., HD_C:], lam, None)
        o = jnp.einsum('bhqk,bkhv->bqhv', p, vv)
    o = rms_norm(o.astype(jnp.float32), norm_w) * (1.0 - lam_init)
    return o.reshape(B, T, W_C).astype(h.dtype) @ w_out, k, v


def conv_ffn(h, hist, w_up, conv_w, conv_b, w_down):
    B, T, _ = h.shape
    u = h @ w_up
    if hist is None:
        hist = jnp.zeros((B, CONV_W - 1, u.shape[-1]), u.dtype)
    full = jnp.concatenate([hist.astype(u.dtype), u], axis=1)
    c = conv_b + full[:, 0:T] * conv_w[0]
    for i in range(1, CONV_W):
        c = c + full[:, i:i + T] * conv_w[i]
    gate, val = jnp.split(c, 2, axis=-1)
    y = (jax.nn.silu(gate) * val) @ w_down
    return y, full[:, T:]


def setup_inputs(seed: int = 0) -> dict:
    key = jax.random.key(seed)
    ks = jax.random.split(key, 32)

    def nrm(k, shape, scale):
        return jax.random.normal(k, shape, jnp.float32) * scale

    band_keep = min(BAND_PAST, PAST_LEN)
    return {
        'x_prompt': nrm(ks[0], (BATCH, SEQ, D_MODEL), 1.0),
        'x_sample': nrm(ks[1], (DEC_BATCH, DEC_SEQ, D_MODEL), 1.0),
        'state_gla': nrm(ks[2], (N_EVEN, DEC_BATCH, H_A, DK_A, DV_A), 1.0),
        'cache_band_k': nrm(ks[3], (N_EVEN, DEC_BATCH, band_keep, H_B, HD_B), 1.0),
        'cache_band_v': nrm(ks[4], (N_EVEN, DEC_BATCH, band_keep, H_B, HD_B), 1.0),
        'cache_diff_k': nrm(ks[5], (N_ODD, DEC_BATCH, PAST_LEN, H_C, 2 * HD_C), 1.0),
        'cache_diff_v': nrm(ks[6], (N_ODD, DEC_BATCH, PAST_LEN, H_C, 2 * HD_C), 1.0),
        'state_ffn_conv': nrm(ks[7], (DEPTH, DEC_BATCH, CONV_W - 1, 2 * D_FF), 1.0),
        'norm_mix_w': 1.0 + nrm(ks[8], (DEPTH, D_MODEL), 0.01),
        'norm_ffn_w': 1.0 + nrm(ks[9], (DEPTH, D_MODEL), 0.01),
        'final_norm_w': 1.0 + nrm(ks[10], (D_MODEL,), 0.01),
        'w_in_even': nrm(ks[11], (N_EVEN, D_MODEL, D_IN_EVEN), D_MODEL ** -0.5),
        'w_alpha_up': nrm(ks[12], (N_EVEN, GATE_RANK, H_A * DK_A), GATE_RANK ** -0.5),
        'b_alpha': nrm(ks[13], (N_EVEN, H_A * DK_A), 0.01),
        'gla_norm_w': 1.0 + nrm(ks[14], (N_EVEN, DV_A), 0.01),
        'rel_bias_table': nrm(ks[15], (N_EVEN, H_B, 2 * REL_CLIP + 1), 0.5),
        'w_out_even': nrm(ks[16], (N_EVEN, D_MIX_EVEN, D_MODEL), D_MIX_EVEN ** -0.5),
        'w_in_odd': nrm(ks[17], (N_ODD, D_MODEL, D_IN_ODD), D_MODEL ** -0.5),
        'lambda_q1': nrm(ks[18], (N_ODD, HD_C), 0.1),
        'lambda_k1': nrm(ks[19], (N_ODD, HD_C), 0.1),
        'lambda_q2': nrm(ks[20], (N_ODD, HD_C), 0.1),
        'lambda_k2': nrm(ks[21], (N_ODD, HD_C), 0.1),
        'diff_norm_w': 1.0 + nrm(ks[22], (N_ODD, 2 * HD_C), 0.01),
        'w_out_odd': nrm(ks[23], (N_ODD, W_C, D_MODEL), W_C ** -0.5),
        'w_ffn_up': nrm(ks[24], (DEPTH, D_MODEL, 2 * D_FF), D_MODEL ** -0.5),
        'ffn_conv_w': nrm(ks[25], (DEPTH, CONV_W, 2 * D_FF), CONV_W ** -0.5),
        'ffn_conv_b': nrm(ks[26], (DEPTH, 2 * D_FF), 0.01),
        'w_ffn_down': nrm(ks[27], (DEPTH, D_FF, D_MODEL), D_FF ** -0.5),
    }


def reference(x_prompt, x_sample, state_gla, cache_band_k, cache_band_v, cache_diff_k, cache_diff_v,
              state_ffn_conv, norm_mix_w, norm_ffn_w, final_norm_w, w_in_even, w_alpha_up, b_alpha,
              gla_norm_w, rel_bias_table, w_out_even, w_in_odd, lambda_q1, lambda_k1, lambda_q2, lambda_k2,
              diff_norm_w, w_out_odd, w_ffn_up, ffn_conv_w, ffn_conv_b, w_ffn_down):
    pos_p = jnp.arange(x_prompt.shape[1])
    pos_s = PAST_LEN + jnp.arange(x_sample.shape[1])
    xp, xs = x_prompt, x_sample
    gla_p, gla_s = [], []
    bk_p, bv_p, bk_s, bv_s = [], [], [], []
    dk_p, dv_p, dk_s, dv_s = [], [], [], []
    cv_p, cv_s = [], []
    for layer in range(DEPTH):
        hp = rms_norm(xp, norm_mix_w[layer])
        hs = rms_norm(xs, norm_mix_w[layer])
        if layer % 2 == 0:
            e = layer // 2
            yp, s_p, kr_p, vr_p = even_mixer(hp, None, w_in_even[e], w_alpha_up[e], b_alpha[e],
                                             gla_norm_w[e], rel_bias_table[e], w_out_even[e])
            ys, s_s, kr_s, vr_s = even_mixer(hs, (state_gla[e], cache_band_k[e], cache_band_v[e]),
                                             w_in_even[e], w_alpha_up[e], b_alpha[e],
                                             gla_norm_w[e], rel_bias_table[e], w_out_even[e])
            gla_p.append(s_p)
            gla_s.append(s_s)
            bk_p.append(kr_p)
            bv_p.append(vr_p)
            bk_s.append(kr_s)
            bv_s.append(vr_s)
        else:
            o = layer // 2
            yp, kr_p, vr_p = odd_mixer(hp, pos_p, None, layer, w_in_odd[o], lambda_q1[o], lambda_k1[o],
                                       lambda_q2[o], lambda_k2[o], diff_norm_w[o], w_out_odd[o])
            ys, kr_s, vr_s = odd_mixer(hs, pos_s, (cache_diff_k[o], cache_diff_v[o]), layer, w_in_odd[o],
                                       lambda_q1[o], lambda_k1[o], lambda_q2[o], lambda_k2[o],
                                       diff_norm_w[o], w_out_odd[o])
            dk_p.append(kr_p)
            dv_p.append(vr_p)
            dk_s.append(kr_s)
            dv_s.append(vr_s)
        xp = xp + yp.astype(xp.dtype)
        xs = xs + ys.astype(xs.dtype)
        fp, c_p = conv_ffn(rms_norm(xp, norm_ffn_w[layer]), None, w_ffn_up[layer], ffn_conv_w[layer],
                           ffn_conv_b[layer], w_ffn_down[layer])
        fs, c_s = conv_ffn(rms_norm(xs, norm_ffn_w[layer]), state_ffn_conv[layer], w_ffn_up[layer],
                           ffn_conv_w[layer], ffn_conv_b[layer], w_ffn_down[layer])
        xp = xp + fp.astype(xp.dtype)
        xs = xs + fs.astype(xs.dtype)
        cv_p.append(c_p)
        cv_s.append(c_s)
    y_prompt = rms_norm(xp, final_norm_w)
    y_sample = rms_norm(xs, final_norm_w)
    return (y_prompt, y_sample,
            jnp.stack(gla_p), jnp.stack(gla_s),
            jnp.stack(bk_p), jnp.stack(bv_p), jnp.stack(bk_s), jnp.stack(bv_s),
            jnp.stack(dk_p), jnp.stack(dv_p), jnp.stack(dk_s), jnp.stack(dv_s),
            jnp.stack(cv_p), jnp.stack(cv_s))
```

```python
import functools
import math

import jax
import jax.numpy as jnp
from jax import lax
from jax.experimental import pallas as pl
from jax.experimental.pallas import tpu as pltpu

F32 = jnp.float32
BF16 = jnp.bfloat16

CHUNK = 64
BAND_PAST = 512
REL_CLIP = 128
ROT_DIM = 32
ROT_HALF = ROT_DIM // 2
ROPE_THETA = 500000.0
GATE_RANK = 16
GATE_NORM = 16.0
CONV_W = 3
EPS = 1e-6

LANE = 128
V7X_VMEM_LIMIT_BYTES = 56 * 1024 * 1024
MASKED = -1e30
MATMUL_ROWS = 2048
MATMUL_COLS = 256
FFN_ROW_CHUNK = 256

_NT = (((1,), (1,)), ((), ()))
_TN = (((0,), (0,)), ((), ()))


def _params(*semantics):
    return pltpu.CompilerParams(dimension_semantics=semantics, vmem_limit_bytes=V7X_VMEM_LIMIT_BYTES)


def _silu(x):
    return x / (1.0 + jnp.exp(-x))


def _rmsnorm_body(x_ref, w_ref, o_ref):
    x = x_ref[...]
    inv = lax.rsqrt(jnp.mean(x * x, axis=-1, keepdims=True) + EPS)
    o_ref[...] = (x * inv * w_ref[...]).astype(o_ref.dtype)


def _rmsnorm(x, w, out_dtype):
    m, d = x.shape
    tm = min(m, 256)
    return pl.pallas_call(
        _rmsnorm_body,
        grid=(m // tm,),
        in_specs=[pl.BlockSpec((tm, d), lambda i: (i, 0)), pl.BlockSpec((1, d), lambda i: (0, 0))],
        out_specs=pl.BlockSpec((tm, d), lambda i: (i, 0)),
        out_shape=jax.ShapeDtypeStruct((m, d), out_dtype),
        compiler_params=_params("parallel"),
        name="rmsnorm",
    )(x, w.reshape(1, d))


def _matmul_body(*refs, rope, resid):
    a_ref, w_ref = refs[:2]
    rest = list(refs[2:])
    acc = jnp.dot(a_ref[...], w_ref[...].astype(BF16), preferred_element_type=F32)
    if resid:
        acc = acc + rest.pop(0)[...]
    if rope:
        cos, sin_up, sin_dn = (r[...] for r in rest[:3])
        o_ref = rest[3]
        for g in range(acc.shape[1] // LANE):
            x = acc[:, g * LANE:(g + 1) * LANE]
            y = x * cos + pltpu.roll(x, ROT_HALF, axis=1) * sin_up + pltpu.roll(x, LANE - ROT_HALF, axis=1) * sin_dn
            o_ref[:, g * LANE:(g + 1) * LANE] = y.astype(o_ref.dtype)
    else:
        o_ref = rest[0]
        o_ref[...] = acc.astype(o_ref.dtype)


def _matmul(a, w, layer, out_dtype, *, n=None, w_col=0, tk=None, a_kb=0, w_kb=0, resid=None, rope=None):
    m = a.shape[0]
    n = w.shape[2] if n is None else n
    tk = a.shape[1] if tk is None else tk
    tm = min(m, MATMUL_ROWS)
    tn = min(MATMUL_COLS, n)
    assert m % tm == 0 and n % tn == 0 and w_col % tn == 0 and a.shape[1] % tk == 0 and w.shape[1] % tk == 0
    w_cb = w_col // tn
    in_specs = [pl.BlockSpec((tm, tk), lambda i, j: (i, a_kb), pipeline_mode=pl.Buffered(1)),
                pl.BlockSpec((None, tk, tn), lambda i, j: (layer, w_kb, j + w_cb))]
    args = [a, w]
    if resid is not None:
        in_specs.append(pl.BlockSpec((tm, tn), lambda i, j: (i, j)))
        args.append(resid)
    if rope is not None:
        period = rope[0].shape[0] // tm
        for t in rope:
            in_specs.append(pl.BlockSpec((tm, LANE), lambda i, j: (i % period, 0)))
            args.append(t)
    return pl.pallas_call(
        functools.partial(_matmul_body, rope=rope is not None, resid=resid is not None),
        grid=(m // tm, n // tn),
        in_specs=in_specs,
        out_specs=pl.BlockSpec((tm, tn), lambda i, j: (i, j)),
        out_shape=jax.ShapeDtypeStruct((m, n), out_dtype),
        compiler_params=_params("parallel", "arbitrary"),
        name="matmul",
    )(*args)


def _rope_tables(pos, rows):
    inv_freq = ROPE_THETA ** (-jnp.arange(ROT_HALF, dtype=F32) / ROT_HALF)
    ang = pos.astype(F32)[:, None] * inv_freq[None, :]
    cos, sin = jnp.cos(ang), jnp.sin(ang)
    t = pos.shape[0]
    pad = jnp.zeros((t, LANE - ROT_DIM), F32)
    zero = jnp.zeros((t, ROT_HALF), F32)
    c = jnp.concatenate([cos, cos, pad + 1.0], axis=1)
    s_up = jnp.concatenate([zero, sin, pad], axis=1)
    s_dn = jnp.concatenate([-sin, zero, pad], axis=1)
    rep = rows // t
    return tuple(jnp.tile(x, (rep, 1)) for x in (c, s_up, s_dn))


def _gla_body(*refs, nc, has_s0, dk):
    if has_s0:
        q_ref, k_ref, v_ref, g_ref, r_ref, wup_ref, b_ref, nw_ref, s0_ref, o_ref, sout_ref, st_ref = refs
    else:
        q_ref, k_ref, v_ref, g_ref, r_ref, wup_ref, b_ref, nw_ref, o_ref, sout_ref, st_ref = refs
    c = pl.program_id(2)
    L = q_ref.shape[0]
    SUB = 16

    @pl.when(c == 0)
    def _():
        if has_s0:
            st_ref[...] = s0_ref[0, 0].T
        else:
            st_ref[...] = jnp.zeros_like(st_ref)

    q = q_ref[...].astype(F32) * (dk ** -0.5)
    k = k_ref[...].astype(F32)
    v = v_ref[...]
    vf = v.astype(F32)
    z = jnp.dot(r_ref[...], wup_ref[...], preferred_element_type=F32, precision=lax.Precision.HIGHEST) + b_ref[...]
    log_a = (jnp.minimum(z, 0.0) - jnp.log1p(jnp.exp(-jnp.abs(z)))) * (1.0 / GATE_NORM)
    row = lax.broadcasted_iota(jnp.int32, (L, 1), 0)
    cum = log_a
    s = 1
    while s < L:
        cum = cum + jnp.where(row >= s, pltpu.roll(cum, s, axis=0), 0.0)
        s *= 2

    st = st_ref[...]
    q_dec = (q * jnp.exp(cum)).astype(BF16)
    o_inter = lax.dot_general(q_dec, st.astype(BF16), _NT, preferred_element_type=F32)

    sub_row = lax.broadcasted_iota(jnp.int32, (SUB, 1), 0)
    outs = []
    for blk in range(L // SUB):
        r0 = blk * SUB
        qb, kb, cb, vb = q[r0:r0 + SUB], k[r0:r0 + SUB], cum[r0:r0 + SUB], vf[r0:r0 + SUB]
        acc = o_inter[r0:r0 + SUB]
        if blk > 0:
            c_edge = cum[r0 - 1:r0]
            q_rel = (qb * jnp.exp(cb - c_edge)).astype(BF16)
            k_rel = (k * jnp.exp(jnp.where(row < r0, c_edge - cum, -jnp.inf))).astype(BF16)
            att = lax.dot_general(q_rel, k_rel, _NT, preferred_element_type=F32)
            acc = acc + jnp.dot(att.astype(BF16), v, preferred_element_type=F32)
        for j in range(SUB):
            d = cb - cb[j:j + 1]
            e = jnp.exp(jnp.where(sub_row >= j, d, -jnp.inf))
            a = jnp.sum(qb * e * kb[j:j + 1], axis=1, keepdims=True)
            acc = acc + a * vb[j:j + 1]
        outs.append(acc)
    o = jnp.concatenate(outs, axis=0)

    c_last = cum[L - 1:L]
    k_dec = (k * jnp.exp(c_last - cum)).astype(BF16)
    st_new = st * jnp.exp(c_last) + lax.dot_general(v, k_dec, _TN, preferred_element_type=F32)
    st_ref[...] = st_new

    inv = lax.rsqrt(jnp.mean(o * o, axis=-1, keepdims=True) + EPS)
    o_ref[...] = (o * inv * nw_ref[...] * _silu(g_ref[...].astype(F32))).astype(o_ref.dtype)

    @pl.when(c == nc - 1)
    def _():
        sout_ref[0, 0] = st_new.T


def _gla(pa, r, wup, b_alpha, norm_w, s0, *, batch, heads, dk, dv):
    m = pa.shape[0]
    t = m // batch
    L = min(CHUNK, t)
    nc = t // L
    kq = heads * dk // dk
    kv = 2 * heads * dk // dv
    kg = kv + heads
    row = lambda b, h, c: b * nc + c
    in_specs = [
        pl.BlockSpec((L, dk), lambda b, h, c: (row(b, h, c), h)),
        pl.BlockSpec((L, dk), lambda b, h, c: (row(b, h, c), kq + h)),
        pl.BlockSpec((L, dv), lambda b, h, c: (row(b, h, c), kv + h)),
        pl.BlockSpec((L, dv), lambda b, h, c: (row(b, h, c), kg + h)),
        pl.BlockSpec((L, LANE), lambda b, h, c: (row(b, h, c), 0)),
        pl.BlockSpec((LANE, dk), lambda b, h, c: (0, h)),
        pl.BlockSpec((1, dk), lambda b, h, c: (0, h)),
        pl.BlockSpec((1, dv), lambda b, h, c: (0, 0)),
    ]
    args = [pa, pa, pa, pa, r, wup, b_alpha.reshape(1, -1), norm_w.reshape(1, dv)]
    if s0 is not None:
        in_specs.append(pl.BlockSpec((1, 1, dk, dv), lambda b, h, c: (b, h, 0, 0)))
        args.append(s0)
    return pl.pallas_call(
        functools.partial(_gla_body, nc=nc, has_s0=s0 is not None, dk=dk),
        grid=(batch, heads, nc),
        in_specs=in_specs,
        out_specs=[pl.BlockSpec((L, dv), lambda b, h, c: (row(b, h, c), h)),
                   pl.BlockSpec((1, 1, dk, dv), lambda b, h, c: (b, h, 0, 0))],
        out_shape=[jax.ShapeDtypeStruct((m, heads * dv), BF16),
                   jax.ShapeDtypeStruct((batch, heads, dk, dv), F32)],
        scratch_shapes=[pltpu.VMEM((dv, dk), F32)],
        compiler_params=_params("parallel", "parallel", "arbitrary"),
        name="gla",
    )(*args)


def _toeplitz(lookup_row, tq, keep):
    w = lookup_row.shape[-1]
    return pltpu.roll(jnp.broadcast_to(lookup_row, (tq, w)), 0, axis=1, stride=1, stride_axis=0)[:, :keep]


def _band_softmax_pv(q, kp, vp, kn, vn, bias_p, bias_n, scale):
    s1 = lax.dot_general(q, kp, _NT, preferred_element_type=F32) * scale + bias_p
    s2 = lax.dot_general(q, kn, _NT, preferred_element_type=F32) * scale + bias_n
    mx = jnp.maximum(jnp.max(s1, axis=1, keepdims=True), jnp.max(s2, axis=1, keepdims=True))
    e1 = jnp.exp(s1 - mx)
    e2 = jnp.exp(s2 - mx)
    den = jnp.sum(e1, axis=1, keepdims=True) + jnp.sum(e2, axis=1, keepdims=True)
    o = (jnp.dot(e1.astype(BF16), vp, preferred_element_type=F32)
         + jnp.dot(e2.astype(BF16), vn, preferred_element_type=F32))
    return o * (1.0 / den)


def _band_prompt_body(q_ref, kp_ref, vp_ref, kn_ref, vn_ref, g1_ref, g2_ref, o_ref, bias_ref, *, scale):
    tq = q_ref.shape[0]
    j = pl.program_id(2)

    @pl.when(jnp.logical_and(pl.program_id(1) == 0, j == 0))
    def _():
        qc = lax.broadcasted_iota(jnp.int32, (tq, 1), 0) // CHUNK
        kc1 = lax.broadcasted_iota(jnp.int32, (1, BAND_PAST), 1) // CHUNK
        kc2 = lax.broadcasted_iota(jnp.int32, (1, tq), 1) // CHUNK
        bias_ref[:, :BAND_PAST] = jnp.where(kc1 >= qc, _toeplitz(g1_ref[0], tq, BAND_PAST), MASKED)
        bias_ref[:, BAND_PAST:] = jnp.where(kc2 <= qc, _toeplitz(g2_ref[0], tq, tq), MASKED)

    bias_p = bias_ref[:, :BAND_PAST] + jnp.where(j == 0, MASKED, 0.0)
    o = _band_softmax_pv(q_ref[...], kp_ref[...].astype(BF16), vp_ref[...].astype(BF16),
                         kn_ref[...].astype(BF16), vn_ref[...].astype(BF16), bias_p, bias_ref[:, BAND_PAST:], scale)
    o_ref[...] = o.astype(o_ref.dtype)


def _band_sample_body(q_ref, kn_ref, vn_ref, g1_ref, g2_ref, kh_hbm, vh_hbm, o_ref,
                      kbuf, vbuf, bias_p_ref, bias_n_ref, sem, *, heads, hd, scale):
    b = pl.program_id(0)
    slot = b % 2
    tq = q_ref.shape[0]

    def copies(bb, sl):
        out = []
        for h in range(heads):
            out.append(pltpu.make_async_copy(kh_hbm.at[bb, :, h, :], kbuf.at[sl, h], sem.at[0, sl]))
            out.append(pltpu.make_async_copy(vh_hbm.at[bb, :, h, :], vbuf.at[sl, h], sem.at[1, sl]))
        return out

    @pl.when(b == 0)
    def _():
        for cp in copies(0, 0):
            cp.start()

        def build(h, carry):
            bias_p_ref[h] = _toeplitz(g1_ref[h], tq, BAND_PAST)
            bias_n_ref[h] = _toeplitz(g2_ref[h], tq, tq)
            return carry
        lax.fori_loop(0, heads, build, 0)

    @pl.when(b + 1 < pl.num_programs(0))
    def _():
        for cp in copies(b + 1, 1 - slot):
            cp.start()

    for cp in copies(b, slot):
        cp.wait()

    def head(h, carry):
        cols = pl.ds(pl.multiple_of(h * hd, hd), hd)
        o = _band_softmax_pv(q_ref[:, cols], kbuf[slot, h].astype(BF16), vbuf[slot, h].astype(BF16),
                             kn_ref[:, cols].astype(BF16), vn_ref[:, cols].astype(BF16),
                             bias_p_ref[h], bias_n_ref[h], scale)
        o_ref[:, cols] = o.astype(o_ref.dtype)
        return carry
    lax.fori_loop(0, heads, head, 0)


def _band_lookup_rows(table, tq):
    def f(d):
        return table[:, jnp.clip(d, -REL_CLIP, REL_CLIP) + REL_CLIP]
    n1 = jnp.arange(2 * BAND_PAST)
    g1 = f(jnp.where(n1 <= BAND_PAST, BAND_PAST - n1, REL_CLIP))
    w2 = max(2 * tq, LANE)
    n2 = jnp.arange(w2)
    g2 = f(jnp.where(n2 < w2 // 2, -n2, w2 - n2))
    return g1[:, None, :].astype(F32), g2[:, None, :].astype(F32)


def _band_prompt(qb, kb, vb, table, *, batch, heads, hd):
    m = qb.shape[0]
    tq = BAND_PAST
    nb = m // batch // tq
    g1, g2 = _band_lookup_rows(table, tq)
    cur = lambda h, b, j: (b * nb + j, h)
    prev = lambda h, b, j: (b * nb + jnp.maximum(j - 1, 0), h)
    return pl.pallas_call(
        functools.partial(_band_prompt_body, scale=hd ** -0.5),
        grid=(heads, batch, nb),
        in_specs=[pl.BlockSpec((tq, hd), cur),
                  pl.BlockSpec((tq, hd), prev), pl.BlockSpec((tq, hd), prev),
                  pl.BlockSpec((tq, hd), cur), pl.BlockSpec((tq, hd), cur),
                  pl.BlockSpec((1, 1, g1.shape[-1]), lambda h, b, j: (h, 0, 0)),
                  pl.BlockSpec((1, 1, g2.shape[-1]), lambda h, b, j: (h, 0, 0))],
        out_specs=pl.BlockSpec((tq, hd), cur),
        out_shape=jax.ShapeDtypeStruct((m, heads * hd), BF16),
        scratch_shapes=[pltpu.VMEM((tq, BAND_PAST + tq), F32)],
        compiler_params=_params("arbitrary", "arbitrary", "arbitrary"),
        name="band_prompt",
    )(qb, kb, vb, kb, vb, g1, g2)


def _band_sample(qb, kb, vb, k_hist, v_hist, table, *, batch, heads, hd):
    m = qb.shape[0]
    tq = m // batch
    keep = k_hist.shape[1]
    assert keep == BAND_PAST and tq == CHUNK
    g1, g2 = _band_lookup_rows(table, tq)
    cur = pl.BlockSpec((tq, heads * hd), lambda b: (b, 0))
    return pl.pallas_call(
        functools.partial(_band_sample_body, heads=heads, hd=hd, scale=hd ** -0.5),
        grid=(batch,),
        in_specs=[cur, cur, cur,
                  pl.BlockSpec(g1.shape, lambda b: (0, 0, 0)), pl.BlockSpec(g2.shape, lambda b: (0, 0, 0)),
                  pl.BlockSpec(memory_space=pl.ANY), pl.BlockSpec(memory_space=pl.ANY)],
        out_specs=cur,
        out_shape=jax.ShapeDtypeStruct((m, heads * hd), BF16),
        scratch_shapes=[pltpu.VMEM((2, heads, keep, hd), F32), pltpu.VMEM((2, heads, keep, hd), F32),
                        pltpu.VMEM((heads, tq, keep), F32), pltpu.VMEM((heads, tq, tq), F32),
                        pltpu.SemaphoreType.DMA((2, 2))],
        compiler_params=_params("arbitrary"),
        name="band_sample",
    )(qb, kb, vb, g1, g2, k_hist, v_hist)


def _lambda(lq1_ref, lk1_ref, lq2_ref, lk2_ref, lam_init):
    a = jnp.sum(lq1_ref[...] * lk1_ref[...], axis=1, keepdims=True)
    b = jnp.sum(lq2_ref[...] * lk2_ref[...], axis=1, keepdims=True)
    return jnp.exp(a) - jnp.exp(b) + lam_init


def _diff_finish(o, nw_ref, lam_init, o_ref):
    inv = lax.rsqrt(jnp.mean(o * o, axis=-1, keepdims=True) + EPS)
    o_ref[...] = (o * inv * nw_ref[...] * (1.0 - lam_init)).astype(o_ref.dtype)


def _diff_prompt_body(q_ref, k_ref, v_ref, lq1_ref, lk1_ref, lq2_ref, lk2_ref, nw_ref, o_ref,
                      k1_ref, k2_ref, vs_ref, *, nq, lam_init, scale):
    qi = pl.program_id(2)
    tq, hd2 = q_ref.shape
    hd = hd2 // 2

    @pl.when(qi == 0)
    def _():
        k = k_ref[...]
        k1_ref[...] = k[:, :hd].astype(BF16)
        k2_ref[...] = k[:, hd:].astype(BF16)
        vs_ref[...] = v_ref[...].astype(BF16)

    lam = _lambda(lq1_ref, lk1_ref, lq2_ref, lk2_ref, lam_init)
    q = q_ref[...]

    for n in range(nq):
        @pl.when(qi == n)
        def _(n=n):
            nk = tq * (n + 1)
            qc = (n * tq + lax.broadcasted_iota(jnp.int32, (tq, 1), 0)) // CHUNK
            kc = lax.broadcasted_iota(jnp.int32, (1, nk), 1) // CHUNK
            visible = kc <= qc

            def weights(qh, ks_ref):
                s = lax.dot_general(qh, ks_ref[:nk, :], _NT, preferred_element_type=F32) * scale
                s = jnp.where(visible, s, MASKED)
                e = jnp.exp(s - jnp.max(s, axis=1, keepdims=True))
                return e, 1.0 / jnp.sum(e, axis=1, keepdims=True)

            e1, r1 = weights(q[:, :hd], k1_ref)
            e2, r2 = weights(q[:, hd:], k2_ref)
            p = e1 * r1 - e2 * (lam * r2)
            o = jnp.dot(p.astype(BF16), vs_ref[:nk, :], preferred_element_type=F32)
            _diff_finish(o, nw_ref, lam_init, o_ref)


def _diff_prompt(q, k, v, lam_params, norm_w, *, batch, heads, hd, lam_init):
    m = q.shape[0]
    t = m // batch
    tq = 512
    nq = t // tq
    vec = pl.BlockSpec((1, hd), lambda b, h, i: (0, 0))
    return pl.pallas_call(
        functools.partial(_diff_prompt_body, nq=nq, lam_init=lam_init, scale=hd ** -0.5),
        grid=(batch, heads, nq),
        in_specs=[pl.BlockSpec((tq, 2 * hd), lambda b, h, i: (b * nq + i, h)),
                  pl.BlockSpec((t, 2 * hd), lambda b, h, i: (b, h)),
                  pl.BlockSpec((t, 2 * hd), lambda b, h, i: (b, h)),
                  vec, vec, vec, vec,
                  pl.BlockSpec((1, 2 * hd), lambda b, h, i: (0, 0))],
        out_specs=pl.BlockSpec((tq, 2 * hd), lambda b, h, i: (b * nq + i, h)),
        out_shape=jax.ShapeDtypeStruct((m, heads * 2 * hd), BF16),
        scratch_shapes=[pltpu.VMEM((t, hd), BF16), pltpu.VMEM((t, hd), BF16), pltpu.VMEM((t, 2 * hd), BF16)],
        compiler_params=_params("parallel", "parallel", "arbitrary"),
        name="diff_prompt",
    )(q, k, v, *[p.reshape(1, hd) for p in lam_params], norm_w.reshape(1, 2 * hd))


def _head_copies(k_hbm, v_hbm, kbuf, vbuf, sem, b, h, slot):
    return (pltpu.make_async_copy(k_hbm.at[b, :, h, :], kbuf.at[slot], sem.at[0, slot]),
            pltpu.make_async_copy(v_hbm.at[b, :, h, :], vbuf.at[slot], sem.at[1, slot]))


def _diff_sample_body(q_ref, kn_ref, vn_ref, lq1_ref, lk1_ref, lq2_ref, lk2_ref, nw_ref, kc_hbm, vc_hbm, o_ref,
                      kbuf, vbuf, sem, *, lam_init, scale):
    b, h = pl.program_id(0), pl.program_id(1)
    nh = pl.num_programs(1)
    nsteps = pl.num_programs(0) * nh
    step = b * nh + h
    slot = step % 2

    @pl.when(step == 0)
    def _():
        for cp in _head_copies(kc_hbm, vc_hbm, kbuf, vbuf, sem, 0, 0, 0):
            cp.start()

    @pl.when(step + 1 < nsteps)
    def _():
        nxt = step + 1
        for cp in _head_copies(kc_hbm, vc_hbm, kbuf, vbuf, sem, nxt // nh, nxt % nh, 1 - slot):
            cp.start()

    for cp in _head_copies(kc_hbm, vc_hbm, kbuf, vbuf, sem, b, h, slot):
        cp.wait()

    hd = q_ref.shape[1] // 2
    lam = _lambda(lq1_ref, lk1_ref, lq2_ref, lk2_ref, lam_init)
    q = q_ref[...]
    kc = kbuf[slot].astype(BF16)
    kn = kn_ref[...].astype(BF16)

    def weights(lo):
        qh = q[:, lo:lo + hd]
        sa = lax.dot_general(qh, kc[:, lo:lo + hd], _NT, preferred_element_type=F32) * scale
        sb = lax.dot_general(qh, kn[:, lo:lo + hd], _NT, preferred_element_type=F32) * scale
        mx = jnp.maximum(jnp.max(sa, axis=1, keepdims=True), jnp.max(sb, axis=1, keepdims=True))
        ea, eb = jnp.exp(sa - mx), jnp.exp(sb - mx)
        return ea, eb, 1.0 / (jnp.sum(ea, axis=1, keepdims=True) + jnp.sum(eb, axis=1, keepdims=True))

    ea1, eb1, r1 = weights(0)
    ea2, eb2, r2 = weights(hd)
    r2 = lam * r2
    pa = (ea1 * r1 - ea2 * r2).astype(BF16)
    pb = (eb1 * r1 - eb2 * r2).astype(BF16)
    o = (jnp.dot(pa, vbuf[slot].astype(BF16), preferred_element_type=F32)
         + jnp.dot(pb, vn_ref[...].astype(BF16), preferred_element_type=F32))
    _diff_finish(o, nw_ref, lam_init, o_ref)


def _diff_sample(q, k, v, k_hist, v_hist, lam_params, norm_w, *, batch, heads, hd, lam_init):
    m = q.shape[0]
    t = m // batch
    past = k_hist.shape[1]
    width = heads * 2 * hd
    cur = lambda b, h: (b, h)
    vec = pl.BlockSpec((1, hd), lambda b, h: (0, 0))
    return pl.pallas_call(
        functools.partial(_diff_sample_body, lam_init=lam_init, scale=hd ** -0.5),
        grid=(batch, heads),
        in_specs=[pl.BlockSpec((t, 2 * hd), cur), pl.BlockSpec((t, 2 * hd), cur), pl.BlockSpec((t, 2 * hd), cur),
                  vec, vec, vec, vec,
                  pl.BlockSpec((1, 2 * hd), lambda b, h: (0, 0)),
                  pl.BlockSpec(memory_space=pl.ANY), pl.BlockSpec(memory_space=pl.ANY)],
        out_specs=pl.BlockSpec((t, 2 * hd), cur),
        out_shape=jax.ShapeDtypeStruct((m, width), BF16),
        scratch_shapes=[pltpu.VMEM((2, past, 2 * hd), F32), pltpu.VMEM((2, past, 2 * hd), F32),
                        pltpu.SemaphoreType.DMA((2, 2))],
        compiler_params=_params("arbitrary", "arbitrary"),
        name="diff_sample",
    )(q, k, v, *[p.reshape(1, hd) for p in lam_params], norm_w.reshape(1, 2 * hd), k_hist, v_hist)


def _ffn_up_body(*refs, t, has_hist):
    if has_hist:
        a_ref, wg_ref, wv_ref, cwg_ref, cwv_ref, cbg_ref, cbv_ref, hg_ref, hv_ref, act_ref, sg_ref, sv_ref = refs
    else:
        a_ref, wg_ref, wv_ref, cwg_ref, cwv_ref, cbg_ref, cbv_ref, act_ref, sg_ref, sv_ref = refs
        hg_ref = hv_ref = None
    tm = a_ref.shape[0]
    tn = wg_ref.shape[1]
    rc = min(FFN_ROW_CHUNK, tm)
    wg = wg_ref[...].astype(BF16)
    wv = wv_ref[...].astype(BF16)
    row = lax.broadcasted_iota(jnp.int32, (rc, 1), 0)
    starts = list(range(0, rc, t)) if t < rc else [0]
    zero_row = jnp.zeros((1, tn), F32)

    def conv(u, prev_u, r0, h_ref, cw_ref, cb_ref, s_ref):
        u1 = pltpu.roll(u, 1, axis=0)
        u2 = pltpu.roll(u, 2, axis=0)
        for s in starts:
            g = r0 + s
            if g % t == 0:
                hist = None if h_ref is None else h_ref[g // t]
                m2 = zero_row if hist is None else hist[0:1]
                m1 = zero_row if hist is None else hist[1:2]
            else:
                m2, m1 = prev_u[rc - 2:rc - 1], prev_u[rc - 1:rc]
            u1 = jnp.where(row == s, m1, u1)
            u2 = jnp.where(row == s, m2, jnp.where(row == s + 1, m1, u2))
        ends = [s + t for s in starts] if t < rc else ([rc] if (r0 + rc) % t == 0 else [])
        for e in ends:
            s_ref[(r0 + e) // t - 1] = u[e - (CONV_W - 1):e]
        cw = cw_ref[...]
        return cb_ref[...] + u2 * cw[0:1] + u1 * cw[1:2] + u * cw[2:3]

    prev_g = prev_v = None
    for c in range(tm // rc):
        r0 = c * rc
        a = a_ref[r0:r0 + rc, :]
        ug = jnp.dot(a, wg, preferred_element_type=F32)
        uv = jnp.dot(a, wv, preferred_element_type=F32)
        gate = conv(ug, prev_g, r0, hg_ref, cwg_ref, cbg_ref, sg_ref)
        val = conv(uv, prev_v, r0, hv_ref, cwv_ref, cbv_ref, sv_ref)
        act_ref[r0:r0 + rc, :] = (_silu(gate) * val).astype(act_ref.dtype)
        prev_g, prev_v = ug, uv


def _ffn_up(h, w_up, layer, conv_w, conv_b, hist, *, batch, d_ff):
    m, d = h.shape
    t = m // batch
    tm = min(m, MATMUL_ROWS)
    tn = MATMUL_COLS
    rc = min(FFN_ROW_CHUNK, tm)
    assert m % tm == 0 and d_ff % tn == 0 and tm % t == 0, "a row tile must hold whole sequences"
    assert t % rc == 0 or rc % t == 0
    nj = d_ff // tn
    seqs = tm // t
    cb = conv_b.reshape(1, -1)
    state = lambda lo: pl.BlockSpec((seqs, CONV_W - 1, tn), lambda i, j: (i, 0, j + lo))
    in_specs = [pl.BlockSpec((tm, d), lambda i, j: (i, 0), pipeline_mode=pl.Buffered(1)),
                pl.BlockSpec((None, d, tn), lambda i, j: (layer, 0, j)),
                pl.BlockSpec((None, d, tn), lambda i, j: (layer, 0, j + nj)),
                pl.BlockSpec((CONV_W, tn), lambda i, j: (0, j)),
                pl.BlockSpec((CONV_W, tn), lambda i, j: (0, j + nj)),
                pl.BlockSpec((1, tn), lambda i, j: (0, j)),
                pl.BlockSpec((1, tn), lambda i, j: (0, j + nj))]
    args = [h, w_up, w_up, conv_w, conv_w, cb, cb]
    if hist is not None:
        in_specs += [state(0), state(nj)]
        args += [hist, hist]
    return pl.pallas_call(
        functools.partial(_ffn_up_body, t=t, has_hist=hist is not None),
        grid=(m // tm, nj),
        in_specs=in_specs,
        out_specs=[pl.BlockSpec((tm, tn), lambda i, j: (i, j)), state(0), state(0)],
        out_shape=[jax.ShapeDtypeStruct((m, d_ff), BF16),
                   jax.ShapeDtypeStruct((batch, CONV_W - 1, d_ff), F32),
                   jax.ShapeDtypeStruct((batch, CONV_W - 1, d_ff), F32)],
        compiler_params=_params("parallel", "arbitrary"),
        name="ffn_up",
    )(*args)


def _ffn(x, hist, norm_w, layer, w_up, conv_w, conv_b, w_down, *, batch):
    d_ff = w_down.shape[1]
    h = _rmsnorm(x, norm_w, BF16)
    act, sg, sv = _ffn_up(h, w_up, layer, conv_w, conv_b, hist, batch=batch, d_ff=d_ff)
    tk = d_ff // 2
    x = _matmul(act, w_down, layer, F32, tk=tk, a_kb=0, w_kb=0, resid=x)
    x = _matmul(act, w_down, layer, F32, tk=tk, a_kb=1, w_kb=1, resid=x)
    return x, jnp.concatenate([sg, sv], axis=-1)


def _even_layer(x, hist, norm_w, w, *, batch):
    heads_a, dk, dv = w["heads_a"], w["dk"], w["dv"]
    heads_b, hd = w["heads_b"], w["hd"]
    h = _rmsnorm(x, norm_w, BF16)
    nb = heads_b * hd
    e = w["index"]
    pa = _matmul(h, w["w_in"], e, BF16, n=w["na"])
    r = _matmul(h, w["w_r"], 0, F32)
    qb = _matmul(h, w["w_b"], 0, BF16, n=nb)
    kb = _matmul(h, w["w_b"], 0, F32, n=nb, w_col=nb)
    vb = _matmul(h, w["w_b"], 0, F32, n=nb, w_col=2 * nb)
    s0 = None if hist is None else hist[0]
    o_a, state = _gla(pa, r, w["w_alpha_up"], w["b_alpha"], w["gla_norm_w"], s0,
                      batch=batch, heads=heads_a, dk=dk, dv=dv)
    if hist is None:
        o_b = _band_prompt(qb, kb, vb, w["table"], batch=batch, heads=heads_b, hd=hd)
    else:
        o_b = _band_sample(qb, kb, vb, hist[1], hist[2], w["table"], batch=batch, heads=heads_b, hd=hd)
    wa = heads_a * dv
    x = _matmul(o_a, w["w_out"], e, F32, tk=wa, a_kb=0, w_kb=0, resid=x)
    x = _matmul(o_b, w["w_out"], e, F32, tk=heads_b * hd, a_kb=0, w_kb=wa // (heads_b * hd), resid=x)
    return x, state, kb, vb


def _odd_layer(x, pos, hist, norm_w, w, *, batch):
    heads, hd = w["heads"], w["hd"]
    m = x.shape[0]
    h = _rmsnorm(x, norm_w, BF16)
    tm = min(m, MATMUL_ROWS)
    rope = _rope_tables(pos, tm if tm % pos.shape[0] == 0 else pos.shape[0])
    wc = heads * 2 * hd
    o_idx = w["index"]
    q = _matmul(h, w["w_in"], o_idx, BF16, n=wc, rope=rope)
    k = _matmul(h, w["w_in"], o_idx, F32, n=wc, w_col=wc, rope=rope)
    v = _matmul(h, w["w_in"], o_idx, F32, n=wc, w_col=2 * wc)
    lam_params = (w["lq1"], w["lk1"], w["lq2"], w["lk2"])
    if hist is None:
        o = _diff_prompt(q, k, v, lam_params, w["norm_w"], batch=batch, heads=heads, hd=hd, lam_init=w["lam_init"])
    else:
        o = _diff_sample(q, k, v, hist[0], hist[1], lam_params, w["norm_w"],
                         batch=batch, heads=heads, hd=hd, lam_init=w["lam_init"])
    x = _matmul(o, w["w_out"], o_idx, F32, resid=x)
    return x, k, v


def kernel(x_prompt, x_sample, state_gla, cache_band_k, cache_band_v, cache_diff_k, cache_diff_v, state_ffn_conv,
           norm_mix_w, norm_ffn_w, final_norm_w, w_in_even, w_alpha_up, b_alpha, gla_norm_w, rel_bias_table,
           w_out_even, w_in_odd, lambda_q1, lambda_k1, lambda_q2, lambda_k2, diff_norm_w, w_out_odd,
           w_ffn_up, ffn_conv_w, ffn_conv_b, w_ffn_down):
    bp, tp, d = x_prompt.shape
    bs, ts, _ = x_sample.shape
    depth = norm_mix_w.shape[0]
    past_len = cache_diff_k.shape[2]
    heads_a, dk, dv = state_gla.shape[2:]
    heads_b, hd_b = cache_band_k.shape[3:]
    heads_c, hd_c2 = cache_diff_k.shape[3:]
    hd_c = hd_c2 // 2
    band_keep = min(BAND_PAST, tp)

    xp = x_prompt.reshape(bp * tp, d)
    xs = x_sample.reshape(bs * ts, d)
    pos_p = jnp.arange(tp)
    pos_s = past_len + jnp.arange(ts)

    gla_p, gla_s, bk_p, bv_p, bk_s, bv_s = [], [], [], [], [], []
    dk_p, dv_p, dk_s, dv_s, cv_p, cv_s = [], [], [], [], [], []
    for layer in range(depth):
        if layer % 2 == 0:
            e = layer // 2
            w_in = w_in_even[e]
            na = 2 * heads_a * dk + 2 * heads_a * dv
            nb = heads_b * hd_b
            o0 = na + GATE_RANK
            w = dict(
                heads_a=heads_a, dk=dk, dv=dv, heads_b=heads_b, hd=hd_b,
                index=e, w_in=w_in_even, na=na,
                w_r=jnp.pad(w_in[:, na:o0], ((0, 0), (0, LANE - GATE_RANK)))[None],
                w_b=w_in[None, :, o0:o0 + 3 * nb],
                w_alpha_up=jnp.pad(w_alpha_up[e], ((0, LANE - GATE_RANK), (0, 0))),
                b_alpha=b_alpha[e], gla_norm_w=gla_norm_w[e], table=rel_bias_table[e],
                w_out=w_out_even,
            )
            xp, s_p, kp, vp = _even_layer(xp, None, norm_mix_w[layer], w, batch=bp)
            xs, s_s, ks, vs = _even_layer(xs, (state_gla[e], cache_band_k[e], cache_band_v[e]),
                                          norm_mix_w[layer], w, batch=bs)
            gla_p.append(s_p)
            gla_s.append(s_s)
            bk_p.append(kp.reshape(bp, tp, heads_b, hd_b)[:, tp - band_keep:])
            bv_p.append(vp.reshape(bp, tp, heads_b, hd_b)[:, tp - band_keep:])
            bk_s.append(ks.reshape(bs, ts, heads_b, hd_b))
            bv_s.append(vs.reshape(bs, ts, heads_b, hd_b))
        else:
            o = layer // 2
            w = dict(
                heads=heads_c, hd=hd_c, lam_init=0.8 - 0.6 * math.exp(-0.3 * layer),
                index=o, w_in=w_in_odd,
                lq1=lambda_q1[o], lk1=lambda_k1[o], lq2=lambda_q2[o], lk2=lambda_k2[o],
                norm_w=diff_norm_w[o], w_out=w_out_odd,
            )
            xp, kp, vp = _odd_layer(xp, pos_p, None, norm_mix_w[layer], w, batch=bp)
            xs, ks, vs = _odd_layer(xs, pos_s, (cache_diff_k[o], cache_diff_v[o]), norm_mix_w[layer], w, batch=bs)
            dk_p.append(kp.reshape(bp, tp, heads_c, hd_c2))
            dv_p.append(vp.reshape(bp, tp, heads_c, hd_c2))
            dk_s.append(ks.reshape(bs, ts, heads_c, hd_c2))
            dv_s.append(vs.reshape(bs, ts, heads_c, hd_c2))
        xp, c_p = _ffn(xp, None, norm_ffn_w[layer], layer, w_ffn_up, ffn_conv_w[layer], ffn_conv_b[layer],
                       w_ffn_down, batch=bp)
        xs, c_s = _ffn(xs, state_ffn_conv[layer], norm_ffn_w[layer], layer, w_ffn_up, ffn_conv_w[layer],
                       ffn_conv_b[layer], w_ffn_down, batch=bs)
        cv_p.append(c_p)
        cv_s.append(c_s)

    y_prompt = _rmsnorm(xp, final_norm_w, F32).reshape(bp, tp, d)
    y_sample = _rmsnorm(xs, final_norm_w, F32).reshape(bs, ts, d)
    return (y_prompt, y_sample,
            jnp.stack(gla_p), jnp.stack(gla_s),
            jnp.stack(bk_p), jnp.stack(bv_p), jnp.stack(bk_s), jnp.stack(bv_s),
            jnp.stack(dk_p), jnp.stack(dv_p), jnp.stack(dk_s), jnp.stack(dv_s),
            jnp.stack(cv_p), jnp.stack(cv_s))
```

```python
import functools
import math

import jax
import jax.numpy as jnp
from jax import lax
from jax.experimental import pallas as pl
from jax.experimental.pallas import tpu as pltpu

F32 = jnp.float32
BF16 = jnp.bfloat16

CHUNK = 64
BAND_PAST = 512
REL_CLIP = 128
ROT_DIM = 32
ROT_HALF = ROT_DIM // 2
ROPE_THETA = 500000.0
GATE_RANK = 16
GATE_NORM = 16.0
CONV_W = 3
EPS = 1e-6

LANE = 128
V7X_VMEM_LIMIT_BYTES = 56 * 1024 * 1024
MASKED = -1e30
MATMUL_ROWS = 2048
MATMUL_COLS = 512
MATMUL_PANEL_BYTES = 32 * 1024 * 1024
FFN_COLS = 256
BAND_Q_ROWS = 1024
FFN_ROW_CHUNK = 256

_NT = (((1,), (1,)), ((), ()))
_TN = (((0,), (0,)), ((), ()))


def _params(*semantics):
    return pltpu.CompilerParams(dimension_semantics=semantics, vmem_limit_bytes=V7X_VMEM_LIMIT_BYTES)


def _silu(x):
    return x / (1.0 + jnp.exp(-x))


def _rmsnorm_body(x_ref, w_ref, o_ref):
    x = x_ref[...]
    inv = lax.rsqrt(jnp.mean(x * x, axis=-1, keepdims=True) + EPS)
    o_ref[...] = (x * inv * w_ref[...]).astype(o_ref.dtype)


def _rmsnorm(x, w, out_dtype):
    m, d = x.shape
    tm = min(m, 256)
    return pl.pallas_call(
        _rmsnorm_body,
        grid=(m // tm,),
        in_specs=[pl.BlockSpec((tm, d), lambda i: (i, 0)), pl.BlockSpec((1, d), lambda i: (0, 0))],
        out_specs=pl.BlockSpec((tm, d), lambda i: (i, 0)),
        out_shape=jax.ShapeDtypeStruct((m, d), out_dtype),
        compiler_params=_params("parallel"),
        name="rmsnorm",
    )(x, w.reshape(1, d))


def _matmul_body(*refs, rope, resid, w_rows_are_outputs):
    a_ref, w_ref = refs[:2]
    rest = list(refs[2:])
    if w_rows_are_outputs:
        acc = lax.dot_general(a_ref[...], w_ref[...].astype(BF16), _NT, preferred_element_type=F32)
    else:
        acc = jnp.dot(a_ref[...], w_ref[...].astype(BF16), preferred_element_type=F32)
    if resid:
        acc = acc + rest.pop(0)[...]
    if rope:
        cos, sin_up, sin_dn = (r[...] for r in rest[:3])
        o_ref = rest[3]
        for g in range(acc.shape[1] // LANE):
            x = acc[:, g * LANE:(g + 1) * LANE]
            y = x * cos + pltpu.roll(x, ROT_HALF, axis=1) * sin_up + pltpu.roll(x, LANE - ROT_HALF, axis=1) * sin_dn
            o_ref[:, g * LANE:(g + 1) * LANE] = y.astype(o_ref.dtype)
    else:
        o_ref = rest[0]
        o_ref[...] = acc.astype(o_ref.dtype)


def _matmul(a, w, layer, out_dtype, *, n=None, w_col=0, tk=None, a_kb=0, w_kb=0, resid=None, rope=None,
            w_rows_are_outputs=False):
    m = a.shape[0]
    tk = a.shape[1] if tk is None else tk
    tm = min(m, MATMUL_ROWS)
    cols = MATMUL_COLS
    if tm * tk * a.dtype.itemsize + 2 * tk * cols * 4 > MATMUL_PANEL_BYTES:
        cols //= 2
    if w_rows_are_outputs:
        tn = min(cols, n)
        assert m % tm == 0 and n % tn == 0 and w_col % tn == 0 and tk == a.shape[1] == w.shape[1]
        w_rb = w_col // tn
        w_spec = pl.BlockSpec((tn, tk), lambda i, j: (j + w_rb, 0))
    else:
        n = w.shape[2] if n is None else n
        tn = min(cols, n)
        assert m % tm == 0 and n % tn == 0 and w_col % tn == 0 and a.shape[1] % tk == 0 and w.shape[1] % tk == 0
        w_cb = w_col // tn
        w_spec = pl.BlockSpec((None, tk, tn), lambda i, j: (layer, w_kb, j + w_cb))
    in_specs = [pl.BlockSpec((tm, tk), lambda i, j: (i, a_kb), pipeline_mode=pl.Buffered(1)), w_spec]
    args = [a, w]
    if resid is not None:
        in_specs.append(pl.BlockSpec((tm, tn), lambda i, j: (i, j)))
        args.append(resid)
    if rope is not None:
        period = rope[0].shape[0] // tm
        for t in rope:
            in_specs.append(pl.BlockSpec((tm, LANE), lambda i, j: (i % period, 0)))
            args.append(t)
    return pl.pallas_call(
        functools.partial(_matmul_body, rope=rope is not None, resid=resid is not None,
                          w_rows_are_outputs=w_rows_are_outputs),
        grid=(m // tm, n // tn),
        in_specs=in_specs,
        out_specs=pl.BlockSpec((tm, tn), lambda i, j: (i, j)),
        out_shape=jax.ShapeDtypeStruct((m, n), out_dtype),
        compiler_params=_params("parallel", "arbitrary"),
        name="matmul",
    )(*args)


def _rope_tables(pos, rows):
    inv_freq = ROPE_THETA ** (-jnp.arange(ROT_HALF, dtype=F32) / ROT_HALF)
    ang = pos.astype(F32)[:, None] * inv_freq[None, :]
    cos, sin = jnp.cos(ang), jnp.sin(ang)
    t = pos.shape[0]
    pad = jnp.zeros((t, LANE - ROT_DIM), F32)
    zero = jnp.zeros((t, ROT_HALF), F32)
    c = jnp.concatenate([cos, cos, pad + 1.0], axis=1)
    s_up = jnp.concatenate([zero, sin, pad], axis=1)
    s_dn = jnp.concatenate([-sin, zero, pad], axis=1)
    rep = rows // t
    return tuple(jnp.tile(x, (rep, 1)) for x in (c, s_up, s_dn))


def _gla_body(*refs, nc, has_s0, heads, dk, dv):
    if has_s0:
        q_ref, k_ref, v_ref, g_ref, r_ref, wup_ref, b_ref, nw_ref, s0_ref, o_ref, sout_ref, st_ref = refs
    else:
        q_ref, k_ref, v_ref, g_ref, r_ref, wup_ref, b_ref, nw_ref, o_ref, sout_ref, st_ref = refs
    c = pl.program_id(1)
    L = q_ref.shape[0]
    SUB = 16

    @pl.when(c == 0)
    def _():
        for h in range(heads):
            if has_s0:
                st_ref[h] = s0_ref[0, h].T
            else:
                st_ref[h] = jnp.zeros(st_ref.shape[1:], F32)

    z = jnp.dot(r_ref[...], wup_ref[...], preferred_element_type=F32, precision=lax.Precision.HIGHEST) + b_ref[...]
    log_a = (jnp.minimum(z, 0.0) - jnp.log1p(jnp.exp(-jnp.abs(z)))) * (1.0 / GATE_NORM)
    row = lax.broadcasted_iota(jnp.int32, (L, 1), 0)
    cum_all = log_a
    s = 1
    while s < L:
        cum_all = cum_all + jnp.where(row >= s, pltpu.roll(cum_all, s, axis=0), 0.0)
        s *= 2
    sub_row = lax.broadcasted_iota(jnp.int32, (SUB, 1), 0)

    col = lax.broadcasted_iota(jnp.int32, (SUB, L), 1)
    heads_q, heads_k, heads_cum, heads_v, o_inters, att_offs = [], [], [], [], [], []
    for h in range(heads):
        q = q_ref[:, h * dk:(h + 1) * dk].astype(F32) * (dk ** -0.5)
        k = k_ref[:, h * dk:(h + 1) * dk].astype(F32)
        v = v_ref[:, h * dv:(h + 1) * dv]
        cum = cum_all[:, h * dk:(h + 1) * dk]
        st = st_ref[h]
        q_dec = (q * jnp.exp(cum)).astype(BF16)
        o_inters.append(lax.dot_general(q_dec, st.astype(BF16), _NT, preferred_element_type=F32))
        offs = [jnp.zeros((SUB, L), F32)]
        for blk in range(1, L // SUB):
            r0 = blk * SUB
            c_edge = cum[r0 - 1:r0]
            q_rel = (q[r0:r0 + SUB] * jnp.exp(cum[r0:r0 + SUB] - c_edge)).astype(BF16)
            k_rel = (k * jnp.exp(jnp.where(row < r0, c_edge - cum, -jnp.inf))).astype(BF16)
            offs.append(lax.dot_general(q_rel, k_rel, _NT, preferred_element_type=F32))
        c_last = cum[L - 1:L]
        k_dec = (k * jnp.exp(c_last - cum)).astype(BF16)
        st_ref[h] = st * jnp.exp(c_last) + lax.dot_general(v, k_dec, _TN, preferred_element_type=F32)
        heads_q.append(q)
        heads_k.append(k)
        heads_cum.append(cum)
        heads_v.append(v)
        att_offs.append(offs)

    atts = []
    for h in range(heads):
        q, k, cum = heads_q[h], heads_k[h], heads_cum[h]
        rows = []
        for blk in range(L // SUB):
            r0 = blk * SUB
            qb, kb, cb = q[r0:r0 + SUB], k[r0:r0 + SUB], cum[r0:r0 + SUB]
            att = att_offs[h][blk]
            for j in range(SUB):
                d = cb - cb[j:j + 1]
                e = jnp.exp(jnp.where(sub_row >= j, d, -jnp.inf))
                a = jnp.sum(qb * e * kb[j:j + 1], axis=1, keepdims=True)
                att = jnp.where(col == r0 + j, a, att)
            rows.append(att)
        atts.append(jnp.concatenate(rows, axis=0).astype(BF16))

    outs = [o_inters[h] + jnp.dot(atts[h], heads_v[h], preferred_element_type=F32) for h in range(heads)]
    for h in range(heads):
        o = outs[h]
        inv = lax.rsqrt(jnp.mean(o * o, axis=-1, keepdims=True) + EPS)
        gate = _silu(g_ref[:, h * dv:(h + 1) * dv].astype(F32))
        o_ref[:, h * dv:(h + 1) * dv] = (o * inv * nw_ref[...] * gate).astype(o_ref.dtype)

    @pl.when(c == nc - 1)
    def _():
        for h in range(heads):
            sout_ref[0, h] = st_ref[h].T


def _gla(pa, r, wup, b_alpha, norm_w, s0, *, batch, heads, dk, dv):
    m = pa.shape[0]
    t = m // batch
    L = min(CHUNK, t)
    nc = t // L
    wk, wv = heads * dk, heads * dv
    assert wv == 2 * wk, "column blocks below assume [q|k] together as wide as v and as g"
    row = lambda b, c: b * nc + c
    in_specs = [
        pl.BlockSpec((L, wk), lambda b, c: (row(b, c), 0)),
        pl.BlockSpec((L, wk), lambda b, c: (row(b, c), 1)),
        pl.BlockSpec((L, wv), lambda b, c: (row(b, c), 1)),
        pl.BlockSpec((L, wv), lambda b, c: (row(b, c), 2)),
        pl.BlockSpec((L, LANE), lambda b, c: (row(b, c), 0)),
        pl.BlockSpec((LANE, wk), lambda b, c: (0, 0)),
        pl.BlockSpec((1, wk), lambda b, c: (0, 0)),
        pl.BlockSpec((1, dv), lambda b, c: (0, 0)),
    ]
    args = [pa, pa, pa, pa, r, wup, b_alpha.reshape(1, -1), norm_w.reshape(1, dv)]
    if s0 is not None:
        in_specs.append(pl.BlockSpec((1, heads, dk, dv), lambda b, c: (b, 0, 0, 0)))
        args.append(s0)
    return pl.pallas_call(
        functools.partial(_gla_body, nc=nc, has_s0=s0 is not None, heads=heads, dk=dk, dv=dv),
        grid=(batch, nc),
        in_specs=in_specs,
        out_specs=[pl.BlockSpec((L, wv), lambda b, c: (row(b, c), 0)),
                   pl.BlockSpec((1, heads, dk, dv), lambda b, c: (b, 0, 0, 0))],
        out_shape=[jax.ShapeDtypeStruct((m, wv), BF16),
                   jax.ShapeDtypeStruct((batch, heads, dk, dv), F32)],
        scratch_shapes=[pltpu.VMEM((heads, dv, dk), F32)],
        compiler_params=_params("parallel", "arbitrary"),
        name="gla",
    )(*args)


def _toeplitz(lookup_row, tq, keep):
    w = lookup_row.shape[-1]
    return pltpu.roll(jnp.broadcast_to(lookup_row, (tq, w)), 0, axis=1, stride=1, stride_axis=0)[:, :keep]


def _band_softmax_pv(q, kp, vp, kn, vn, bias_p, bias_n, scale):
    s1 = lax.dot_general(q, kp, _NT, preferred_element_type=F32) * scale + bias_p
    s2 = lax.dot_general(q, kn, _NT, preferred_element_type=F32) * scale + bias_n
    mx = jnp.maximum(jnp.max(s1, axis=1, keepdims=True), jnp.max(s2, axis=1, keepdims=True))
    e1 = jnp.exp(s1 - mx)
    e2 = jnp.exp(s2 - mx)
    den = jnp.sum(e1, axis=1, keepdims=True) + jnp.sum(e2, axis=1, keepdims=True)
    o = (jnp.dot(e1.astype(BF16), vp, preferred_element_type=F32)
         + jnp.dot(e2.astype(BF16), vn, preferred_element_type=F32))
    return o * (1.0 / den)


def _band_prompt_body(q_ref, kp_ref, vp_ref, kn_ref, vn_ref, g1_ref, g2_ref, o_ref, bias_ref, *, scale):
    tq = q_ref.shape[0]
    band = BAND_PAST + CHUNK
    j = pl.program_id(2)

    @pl.when(jnp.logical_and(pl.program_id(1) == 0, j == 0))
    def _():
        bias_ref[:, :BAND_PAST] = _toeplitz(g1_ref[0], CHUNK, BAND_PAST)
        bias_ref[:, BAND_PAST:] = _toeplitz(g2_ref[0], CHUNK, CHUNK)

    k_all = jnp.concatenate([kp_ref[...].astype(BF16), kn_ref[...].astype(BF16)], axis=0)
    v_all = jnp.concatenate([vp_ref[...].astype(BF16), vn_ref[...].astype(BF16)], axis=0)
    bias = bias_ref[...]
    col = lax.broadcasted_iota(jnp.int32, (1, band), 1)
    chunks = range(0, tq, CHUNK)
    scores = [lax.dot_general(q_ref[r0:r0 + CHUNK, :], k_all[r0:r0 + band], _NT, preferred_element_type=F32)
              for r0 in chunks]
    probs = []
    for r0, s in zip(chunks, scores):
        s = s * scale + bias
        if r0 < BAND_PAST:
            s = s + jnp.where(jnp.logical_and(j == 0, col < BAND_PAST - r0), MASKED, 0.0)
        e = jnp.exp(s - jnp.max(s, axis=1, keepdims=True))
        probs.append((e.astype(BF16), 1.0 / jnp.sum(e, axis=1, keepdims=True)))
    outs = [jnp.dot(e, v_all[r0:r0 + band], preferred_element_type=F32) for r0, (e, _) in zip(chunks, probs)]
    for r0, o, (_, inv) in zip(chunks, outs, probs):
        o_ref[r0:r0 + CHUNK, :] = (o * inv).astype(o_ref.dtype)


def _band_sample_body(q_ref, kn_ref, vn_ref, g1_ref, g2_ref, kh_hbm, vh_hbm, o_ref,
                      kbuf, vbuf, bias_p_ref, bias_n_ref, sem, *, heads, hd, scale):
    b = pl.program_id(0)
    slot = b % 2
    tq = q_ref.shape[0]

    def copies(bb, sl):
        out = []
        for h in range(heads):
            out.append(pltpu.make_async_copy(kh_hbm.at[bb, :, h, :], kbuf.at[sl, h], sem.at[0, sl]))
            out.append(pltpu.make_async_copy(vh_hbm.at[bb, :, h, :], vbuf.at[sl, h], sem.at[1, sl]))
        return out

    @pl.when(b == 0)
    def _():
        for cp in copies(0, 0):
            cp.start()

        def build(h, carry):
            bias_p_ref[h] = _toeplitz(g1_ref[h], tq, BAND_PAST)
            bias_n_ref[h] = _toeplitz(g2_ref[h], tq, tq)
            return carry
        lax.fori_loop(0, heads, build, 0)

    @pl.when(b + 1 < pl.num_programs(0))
    def _():
        for cp in copies(b + 1, 1 - slot):
            cp.start()

    for cp in copies(b, slot):
        cp.wait()

    def head(h, carry):
        cols = pl.ds(pl.multiple_of(h * hd, hd), hd)
        o = _band_softmax_pv(q_ref[:, cols], kbuf[slot, h].astype(BF16), vbuf[slot, h].astype(BF16),
                             kn_ref[:, cols].astype(BF16), vn_ref[:, cols].astype(BF16),
                             bias_p_ref[h], bias_n_ref[h], scale)
        o_ref[:, cols] = o.astype(o_ref.dtype)
        return carry
    lax.fori_loop(0, heads, head, 0)


def _band_lookup_rows(table, tq):
    def f(d):
        return table[:, jnp.clip(d, -REL_CLIP, REL_CLIP) + REL_CLIP]
    n1 = jnp.arange(2 * BAND_PAST)
    g1 = f(jnp.where(n1 <= BAND_PAST, BAND_PAST - n1, REL_CLIP))
    w2 = max(2 * tq, LANE)
    n2 = jnp.arange(w2)
    g2 = f(jnp.where(n2 < w2 // 2, -n2, w2 - n2))
    return g1[:, None, :].astype(F32), g2[:, None, :].astype(F32)


def _band_prompt(qb, kb, vb, table, *, batch, heads, hd):
    m = qb.shape[0]
    t = m // batch
    tq = min(BAND_Q_ROWS, t)
    assert t % tq == 0 and tq % BAND_PAST == 0
    nb = t // tq
    per = tq // BAND_PAST
    g1, g2 = _band_lookup_rows(table, CHUNK)
    cur = lambda h, b, j: (b * nb + j, h)
    prev = lambda h, b, j: (jnp.maximum((b * nb + j) * per - 1, 0), h)
    return pl.pallas_call(
        functools.partial(_band_prompt_body, scale=hd ** -0.5),
        grid=(heads, batch, nb),
        in_specs=[pl.BlockSpec((tq, hd), cur),
                  pl.BlockSpec((BAND_PAST, hd), prev), pl.BlockSpec((BAND_PAST, hd), prev),
                  pl.BlockSpec((tq, hd), cur), pl.BlockSpec((tq, hd), cur),
                  pl.BlockSpec((1, 1, g1.shape[-1]), lambda h, b, j: (h, 0, 0)),
                  pl.BlockSpec((1, 1, g2.shape[-1]), lambda h, b, j: (h, 0, 0))],
        out_specs=pl.BlockSpec((tq, hd), cur),
        out_shape=jax.ShapeDtypeStruct((m, heads * hd), BF16),
        scratch_shapes=[pltpu.VMEM((CHUNK, BAND_PAST + CHUNK), F32)],
        compiler_params=_params("arbitrary", "arbitrary", "arbitrary"),
        name="band_prompt",
    )(qb, kb, vb, kb, vb, g1, g2)


def _band_sample(qb, kb, vb, k_hist, v_hist, table, *, batch, heads, hd):
    m = qb.shape[0]
    tq = m // batch
    keep = k_hist.shape[1]
    assert keep == BAND_PAST and tq == CHUNK
    g1, g2 = _band_lookup_rows(table, tq)
    cur = pl.BlockSpec((tq, heads * hd), lambda b: (b, 0))
    return pl.pallas_call(
        functools.partial(_band_sample_body, heads=heads, hd=hd, scale=hd ** -0.5),
        grid=(batch,),
        in_specs=[cur, cur, cur,
                  pl.BlockSpec(g1.shape, lambda b: (0, 0, 0)), pl.BlockSpec(g2.shape, lambda b: (0, 0, 0)),
                  pl.BlockSpec(memory_space=pl.ANY), pl.BlockSpec(memory_space=pl.ANY)],
        out_specs=cur,
        out_shape=jax.ShapeDtypeStruct((m, heads * hd), BF16),
        scratch_shapes=[pltpu.VMEM((2, heads, keep, hd), F32), pltpu.VMEM((2, heads, keep, hd), F32),
                        pltpu.VMEM((heads, tq, keep), F32), pltpu.VMEM((heads, tq, tq), F32),
                        pltpu.SemaphoreType.DMA((2, 2))],
        compiler_params=_params("arbitrary"),
        name="band_sample",
    )(qb, kb, vb, g1, g2, k_hist, v_hist)


def _lambda(lq1_ref, lk1_ref, lq2_ref, lk2_ref, lam_init):
    a = jnp.sum(lq1_ref[...] * lk1_ref[...], axis=1, keepdims=True)
    b = jnp.sum(lq2_ref[...] * lk2_ref[...], axis=1, keepdims=True)
    return jnp.exp(a) - jnp.exp(b) + lam_init


def _diff_finish(o, nw_ref, lam_init, o_ref):
    inv = lax.rsqrt(jnp.mean(o * o, axis=-1, keepdims=True) + EPS)
    o_ref[...] = (o * inv * nw_ref[...] * (1.0 - lam_init)).astype(o_ref.dtype)


def _diff_prompt_body(q_ref, k_ref, v_ref, lq1_ref, lk1_ref, lq2_ref, lk2_ref, nw_ref, o_ref,
                      k1_ref, k2_ref, vs_ref, *, nq, lam_init, scale):
    qi = pl.program_id(2)
    tq, hd2 = q_ref.shape
    hd = hd2 // 2

    @pl.when(qi == 0)
    def _():
        k = k_ref[...]
        k1_ref[...] = k[:, :hd].astype(BF16)
        k2_ref[...] = k[:, hd:].astype(BF16)
        vs_ref[...] = v_ref[...].astype(BF16)

    lam = _lambda(lq1_ref, lk1_ref, lq2_ref, lk2_ref, lam_init)
    q = q_ref[...]

    for n in range(nq):
        @pl.when(qi == n)
        def _(n=n):
            nk = tq * (n + 1)
            qc = (n * tq + lax.broadcasted_iota(jnp.int32, (tq, 1), 0)) // CHUNK
            kc = lax.broadcasted_iota(jnp.int32, (1, nk), 1) // CHUNK
            visible = kc <= qc

            s1 = lax.dot_general(q[:, :hd], k1_ref[:nk, :], _NT, preferred_element_type=F32)
            s2 = lax.dot_general(q[:, hd:], k2_ref[:nk, :], _NT, preferred_element_type=F32)

            def weights(s):
                s = jnp.where(visible, s * scale, MASKED)
                e = jnp.exp(s - jnp.max(s, axis=1, keepdims=True))
                return e, 1.0 / jnp.sum(e, axis=1, keepdims=True)

            e1, r1 = weights(s1)
            e2, r2 = weights(s2)
            p = e1 * r1 - e2 * (lam * r2)
            o = jnp.dot(p.astype(BF16), vs_ref[:nk, :], preferred_element_type=F32)
            _diff_finish(o, nw_ref, lam_init, o_ref)


def _diff_prompt(q, k, v, lam_params, norm_w, *, batch, heads, hd, lam_init):
    m = q.shape[0]
    t = m // batch
    tq = 512
    nq = t // tq
    vec = pl.BlockSpec((1, hd), lambda b, h, i: (0, 0))
    return pl.pallas_call(
        functools.partial(_diff_prompt_body, nq=nq, lam_init=lam_init, scale=hd ** -0.5),
        grid=(batch, heads, nq),
        in_specs=[pl.BlockSpec((tq, 2 * hd), lambda b, h, i: (b * nq + i, h)),
                  pl.BlockSpec((t, 2 * hd), lambda b, h, i: (b, h)),
                  pl.BlockSpec((t, 2 * hd), lambda b, h, i: (b, h)),
                  vec, vec, vec, vec,
                  pl.BlockSpec((1, 2 * hd), lambda b, h, i: (0, 0))],
        out_specs=pl.BlockSpec((tq, 2 * hd), lambda b, h, i: (b * nq + i, h)),
        out_shape=jax.ShapeDtypeStruct((m, heads * 2 * hd), BF16),
        scratch_shapes=[pltpu.VMEM((t, hd), BF16), pltpu.VMEM((t, hd), BF16), pltpu.VMEM((t, 2 * hd), BF16)],
        compiler_params=_params("parallel", "parallel", "arbitrary"),
        name="diff_prompt",
    )(q, k, v, *[p.reshape(1, hd) for p in lam_params], norm_w.reshape(1, 2 * hd))


def _head_copies(k_hbm, v_hbm, kbuf, vbuf, sem, b, h, slot):
    return (pltpu.make_async_copy(k_hbm.at[b, :, h, :], kbuf.at[slot], sem.at[0, slot]),
            pltpu.make_async_copy(v_hbm.at[b, :, h, :], vbuf.at[slot], sem.at[1, slot]))


def _diff_sample_body(q_ref, kn_ref, vn_ref, lq1_ref, lk1_ref, lq2_ref, lk2_ref, nw_ref, kc_hbm, vc_hbm, o_ref,
                      kbuf, vbuf, sem, *, lam_init, scale):
    b, h = pl.program_id(0), pl.program_id(1)
    nh = pl.num_programs(1)
    nsteps = pl.num_programs(0) * nh
    step = b * nh + h
    slot = step % 2

    @pl.when(step == 0)
    def _():
        for cp in _head_copies(kc_hbm, vc_hbm, kbuf, vbuf, sem, 0, 0, 0):
            cp.start()

    @pl.when(step + 1 < nsteps)
    def _():
        nxt = step + 1
        for cp in _head_copies(kc_hbm, vc_hbm, kbuf, vbuf, sem, nxt // nh, nxt % nh, 1 - slot):
            cp.start()

    for cp in _head_copies(kc_hbm, vc_hbm, kbuf, vbuf, sem, b, h, slot):
        cp.wait()

    hd = q_ref.shape[1] // 2
    lam = _lambda(lq1_ref, lk1_ref, lq2_ref, lk2_ref, lam_init)
    q = q_ref[...]
    kc = kbuf[slot].astype(BF16)
    kn = kn_ref[...].astype(BF16)

    def weights(lo):
        qh = q[:, lo:lo + hd]
        sa = lax.dot_general(qh, kc[:, lo:lo + hd], _NT, preferred_element_type=F32) * scale
        sb = lax.dot_general(qh, kn[:, lo:lo + hd], _NT, preferred_element_type=F32) * scale
        mx = jnp.maximum(jnp.max(sa, axis=1, keepdims=True), jnp.max(sb, axis=1, keepdims=True))
        ea, eb = jnp.exp(sa - mx), jnp.exp(sb - mx)
        return ea, eb, 1.0 / (jnp.sum(ea, axis=1, keepdims=True) + jnp.sum(eb, axis=1, keepdims=True))

    ea1, eb1, r1 = weights(0)
    ea2, eb2, r2 = weights(hd)
    r2 = lam * r2
    pa = (ea1 * r1 - ea2 * r2).astype(BF16)
    pb = (eb1 * r1 - eb2 * r2).astype(BF16)
    o = (jnp.dot(pa, vbuf[slot].astype(BF16), preferred_element_type=F32)
         + jnp.dot(pb, vn_ref[...].astype(BF16), preferred_element_type=F32))
    _diff_finish(o, nw_ref, lam_init, o_ref)


def _diff_sample(q, k, v, k_hist, v_hist, lam_params, norm_w, *, batch, heads, hd, lam_init):
    m = q.shape[0]
    t = m // batch
    past = k_hist.shape[1]
    width = heads * 2 * hd
    cur = lambda b, h: (b, h)
    vec = pl.BlockSpec((1, hd), lambda b, h: (0, 0))
    return pl.pallas_call(
        functools.partial(_diff_sample_body, lam_init=lam_init, scale=hd ** -0.5),
        grid=(batch, heads),
        in_specs=[pl.BlockSpec((t, 2 * hd), cur), pl.BlockSpec((t, 2 * hd), cur), pl.BlockSpec((t, 2 * hd), cur),
                  vec, vec, vec, vec,
                  pl.BlockSpec((1, 2 * hd), lambda b, h: (0, 0)),
                  pl.BlockSpec(memory_space=pl.ANY), pl.BlockSpec(memory_space=pl.ANY)],
        out_specs=pl.BlockSpec((t, 2 * hd), cur),
        out_shape=jax.ShapeDtypeStruct((m, width), BF16),
        scratch_shapes=[pltpu.VMEM((2, past, 2 * hd), F32), pltpu.VMEM((2, past, 2 * hd), F32),
                        pltpu.SemaphoreType.DMA((2, 2))],
        compiler_params=_params("arbitrary", "arbitrary"),
        name="diff_sample",
    )(q, k, v, *[p.reshape(1, hd) for p in lam_params], norm_w.reshape(1, 2 * hd), k_hist, v_hist)


def _ffn_up_body(*refs, t, has_hist):
    if has_hist:
        a_ref, wg_ref, wv_ref, cwg_ref, cwv_ref, cbg_ref, cbv_ref, hg_ref, hv_ref, act_ref, sg_ref, sv_ref = refs
    else:
        a_ref, wg_ref, wv_ref, cwg_ref, cwv_ref, cbg_ref, cbv_ref, act_ref, sg_ref, sv_ref = refs
        hg_ref = hv_ref = None
    tm = a_ref.shape[0]
    tn = wg_ref.shape[1]
    rc = min(FFN_ROW_CHUNK, tm)
    wg = wg_ref[...].astype(BF16)
    wv = wv_ref[...].astype(BF16)
    row = lax.broadcasted_iota(jnp.int32, (rc, 1), 0)
    starts = list(range(0, rc, t)) if t < rc else [0]
    zero_row = jnp.zeros((1, tn), F32)

    def conv(u, prev_u, r0, h_ref, cw_ref, cb_ref, s_ref):
        u1 = pltpu.roll(u, 1, axis=0)
        u2 = pltpu.roll(u, 2, axis=0)
        for s in starts:
            g = r0 + s
            if g % t == 0:
                hist = None if h_ref is None else h_ref[g // t]
                m2 = zero_row if hist is None else hist[0:1]
                m1 = zero_row if hist is None else hist[1:2]
            else:
                m2, m1 = prev_u[rc - 2:rc - 1], prev_u[rc - 1:rc]
            u1 = jnp.where(row == s, m1, u1)
            u2 = jnp.where(row == s, m2, jnp.where(row == s + 1, m1, u2))
        ends = [s + t for s in starts] if t < rc else ([rc] if (r0 + rc) % t == 0 else [])
        for e in ends:
            s_ref[(r0 + e) // t - 1] = u[e - (CONV_W - 1):e]
        cw = cw_ref[...]
        return cb_ref[...] + u2 * cw[0:1] + u1 * cw[1:2] + u * cw[2:3]

    prev_g = prev_v = None
    for c in range(tm // rc):
        r0 = c * rc
        a = a_ref[r0:r0 + rc, :]
        ug = jnp.dot(a, wg, preferred_element_type=F32)
        uv = jnp.dot(a, wv, preferred_element_type=F32)
        gate = conv(ug, prev_g, r0, hg_ref, cwg_ref, cbg_ref, sg_ref)
        val = conv(uv, prev_v, r0, hv_ref, cwv_ref, cbv_ref, sv_ref)
        act_ref[r0:r0 + rc, :] = (_silu(gate) * val).astype(act_ref.dtype)
        prev_g, prev_v = ug, uv


def _ffn_up(h, w_up, layer, conv_w, conv_b, hist, *, batch, d_ff):
    m, d = h.shape
    t = m // batch
    tm = min(m, MATMUL_ROWS)
    tn = FFN_COLS
    rc = min(FFN_ROW_CHUNK, tm)
    assert m % tm == 0 and d_ff % tn == 0 and tm % t == 0, "a row tile must hold whole sequences"
    assert t % rc == 0 or rc % t == 0
    nj = d_ff // tn
    seqs = tm // t
    cb = conv_b.reshape(1, -1)
    state = lambda lo: pl.BlockSpec((seqs, CONV_W - 1, tn), lambda i, j: (i, 0, j + lo))
    in_specs = [pl.BlockSpec((tm, d), lambda i, j: (i, 0), pipeline_mode=pl.Buffered(1)),
                pl.BlockSpec((None, d, tn), lambda i, j: (layer, 0, j)),
                pl.BlockSpec((None, d, tn), lambda i, j: (layer, 0, j + nj)),
                pl.BlockSpec((CONV_W, tn), lambda i, j: (0, j)),
                pl.BlockSpec((CONV_W, tn), lambda i, j: (0, j + nj)),
                pl.BlockSpec((1, tn), lambda i, j: (0, j)),
                pl.BlockSpec((1, tn), lambda i, j: (0, j + nj))]
    args = [h, w_up, w_up, conv_w, conv_w, cb, cb]
    if hist is not None:
        in_specs += [state(0), state(nj)]
        args += [hist, hist]
    return pl.pallas_call(
        functools.partial(_ffn_up_body, t=t, has_hist=hist is not None),
        grid=(m // tm, nj),
        in_specs=in_specs,
        out_specs=[pl.BlockSpec((tm, tn), lambda i, j: (i, j)), state(0), state(0)],
        out_shape=[jax.ShapeDtypeStruct((m, d_ff), BF16),
                   jax.ShapeDtypeStruct((batch, CONV_W - 1, d_ff), F32),
                   jax.ShapeDtypeStruct((batch, CONV_W - 1, d_ff), F32)],
        compiler_params=_params("parallel", "arbitrary"),
        name="ffn_up",
    )(*args)


def _ffn(x, hist, norm_w, layer, w_up, conv_w, conv_b, w_down, *, batch):
    d_ff = w_down.shape[1]
    h = _rmsnorm(x, norm_w, BF16)
    act, sg, sv = _ffn_up(h, w_up, layer, conv_w, conv_b, hist, batch=batch, d_ff=d_ff)
    tk = d_ff // 2
    x = _matmul(act, w_down, layer, F32, tk=tk, a_kb=0, w_kb=0, resid=x)
    x = _matmul(act, w_down, layer, F32, tk=tk, a_kb=1, w_kb=1, resid=x)
    return x, jnp.concatenate([sg, sv], axis=-1)


def _even_layer(x, hist, norm_w, w, *, batch):
    heads_a, dk, dv = w["heads_a"], w["dk"], w["dv"]
    heads_b, hd = w["heads_b"], w["hd"]
    h = _rmsnorm(x, norm_w, BF16)
    nb = heads_b * hd
    e = w["index"]
    na = w["na"]
    mm_a = functools.partial(_matmul, h, w["w_in_t"], 0, w_rows_are_outputs=True)
    mm_b = functools.partial(_matmul, h, w["w_band_t"], 0, w_rows_are_outputs=True)
    pa = mm_a(BF16, n=na)
    r = mm_a(F32, n=LANE, w_col=na)
    qb = mm_b(BF16, n=nb)
    kb = mm_b(F32, n=nb, w_col=nb)
    vb = mm_b(F32, n=nb, w_col=2 * nb)
    s0 = None if hist is None else hist[0]
    o_a, state = _gla(pa, r, w["w_alpha_up"], w["b_alpha"], w["gla_norm_w"], s0,
                      batch=batch, heads=heads_a, dk=dk, dv=dv)
    if hist is None:
        o_b = _band_prompt(qb, kb, vb, w["table"], batch=batch, heads=heads_b, hd=hd)
    else:
        o_b = _band_sample(qb, kb, vb, hist[1], hist[2], w["table"], batch=batch, heads=heads_b, hd=hd)
    wa = heads_a * dv
    x = _matmul(o_a, w["w_out"], e, F32, tk=wa, a_kb=0, w_kb=0, resid=x)
    x = _matmul(o_b, w["w_out"], e, F32, tk=heads_b * hd, a_kb=0, w_kb=wa // (heads_b * hd), resid=x)
    return x, state, kb, vb


def _odd_layer(x, pos, hist, norm_w, w, *, batch):
    heads, hd = w["heads"], w["hd"]
    m = x.shape[0]
    h = _rmsnorm(x, norm_w, BF16)
    tm = min(m, MATMUL_ROWS)
    rope = _rope_tables(pos, tm if tm % pos.shape[0] == 0 else pos.shape[0])
    wc = heads * 2 * hd
    o_idx = w["index"]
    q = _matmul(h, w["w_in"], o_idx, BF16, n=wc, rope=rope)
    k = _matmul(h, w["w_in"], o_idx, F32, n=wc, w_col=wc, rope=rope)
    v = _matmul(h, w["w_in"], o_idx, F32, n=wc, w_col=2 * wc)
    lam_params = (w["lq1"], w["lk1"], w["lq2"], w["lk2"])
    if hist is None:
        o = _diff_prompt(q, k, v, lam_params, w["norm_w"], batch=batch, heads=heads, hd=hd, lam_init=w["lam_init"])
    else:
        o = _diff_sample(q, k, v, hist[0], hist[1], lam_params, w["norm_w"],
                         batch=batch, heads=heads, hd=hd, lam_init=w["lam_init"])
    x = _matmul(o, w["w_out"], o_idx, F32, resid=x)
    return x, k, v


def kernel(x_prompt, x_sample, state_gla, cache_band_k, cache_band_v, cache_diff_k, cache_diff_v, state_ffn_conv,
           norm_mix_w, norm_ffn_w, final_norm_w, w_in_even, w_alpha_up, b_alpha, gla_norm_w, rel_bias_table,
           w_out_even, w_in_odd, lambda_q1, lambda_k1, lambda_q2, lambda_k2, diff_norm_w, w_out_odd,
           w_ffn_up, ffn_conv_w, ffn_conv_b, w_ffn_down):
    bp, tp, d = x_prompt.shape
    bs, ts, _ = x_sample.shape
    depth = norm_mix_w.shape[0]
    past_len = cache_diff_k.shape[2]
    heads_a, dk, dv = state_gla.shape[2:]
    heads_b, hd_b = cache_band_k.shape[3:]
    heads_c, hd_c2 = cache_diff_k.shape[3:]
    hd_c = hd_c2 // 2
    band_keep = min(BAND_PAST, tp)

    xp = x_prompt.reshape(bp * tp, d)
    xs = x_sample.reshape(bs * ts, d)
    pos_p = jnp.arange(tp)
    pos_s = past_len + jnp.arange(ts)

    gla_p, gla_s, bk_p, bv_p, bk_s, bv_s = [], [], [], [], [], []
    dk_p, dv_p, dk_s, dv_s, cv_p, cv_s = [], [], [], [], [], []
    for layer in range(depth):
        if layer % 2 == 0:
            e = layer // 2
            na = 2 * heads_a * dk + 2 * heads_a * dv
            w_in_t = jnp.swapaxes(w_in_even[e], 0, 1)
            w = dict(
                heads_a=heads_a, dk=dk, dv=dv, heads_b=heads_b, hd=hd_b,
                index=e, w_in_t=w_in_t, w_band_t=w_in_t[na + GATE_RANK:], na=na,
                w_alpha_up=jnp.pad(w_alpha_up[e], ((0, LANE - GATE_RANK), (0, 0))),
                b_alpha=b_alpha[e], gla_norm_w=gla_norm_w[e], table=rel_bias_table[e],
                w_out=w_out_even,
            )
            xp, s_p, kp, vp = _even_layer(xp, None, norm_mix_w[layer], w, batch=bp)
            xs, s_s, ks, vs = _even_layer(xs, (state_gla[e], cache_band_k[e], cache_band_v[e]),
                                          norm_mix_w[layer], w, batch=bs)
            gla_p.append(s_p)
            gla_s.append(s_s)
            bk_p.append(kp.reshape(bp, tp, heads_b, hd_b)[:, tp - band_keep:])
            bv_p.append(vp.reshape(bp, tp, heads_b, hd_b)[:, tp - band_keep:])
            bk_s.append(ks.reshape(bs, ts, heads_b, hd_b))
            bv_s.append(vs.reshape(bs, ts, heads_b, hd_b))
        else:
            o = layer // 2
            w = dict(
                heads=heads_c, hd=hd_c, lam_init=0.8 - 0.6 * math.exp(-0.3 * layer),
                index=o, w_in=w_in_odd,
                lq1=lambda_q1[o], lk1=lambda_k1[o], lq2=lambda_q2[o], lk2=lambda_k2[o],
                norm_w=diff_norm_w[o], w_out=w_out_odd,
            )
            xp, kp, vp = _odd_layer(xp, pos_p, None, norm_mix_w[layer], w, batch=bp)
            xs, ks, vs = _odd_layer(xs, pos_s, (cache_diff_k[o], cache_diff_v[o]), norm_mix_w[layer], w, batch=bs)
            dk_p.append(kp.reshape(bp, tp, heads_c, hd_c2))
            dv_p.append(vp.reshape(bp, tp, heads_c, hd_c2))
            dk_s.append(ks.reshape(bs, ts, heads_c, hd_c2))
            dv_s.append(vs.reshape(bs, ts, heads_c, hd_c2))
        xp, c_p = _ffn(xp, None, norm_ffn_w[layer], layer, w_ffn_up, ffn_conv_w[layer], ffn_conv_b[layer],
                       w_ffn_down, batch=bp)
        xs, c_s = _ffn(xs, state_ffn_conv[layer], norm_ffn_w[layer], layer, w_ffn_up, ffn_conv_w[layer],
                       ffn_conv_b[layer], w_ffn_down, batch=bs)
        cv_p.append(c_p)
        cv_s.append(c_s)

    y_prompt = _rmsnorm(xp, final_norm_w, F32).reshape(bp, tp, d)
    y_sample = _rmsnorm(xs, final_norm_w, F32).reshape(bs, ts, d)
    return (y_prompt, y_sample,
            jnp.stack(gla_p), jnp.stack(gla_s),
            jnp.stack(bk_p), jnp.stack(bv_p), jnp.stack(bk_s), jnp.stack(bv_s),
            jnp.stack(dk_p), jnp.stack(dv_p), jnp.stack(dk_s), jnp.stack(dv_s),
            jnp.stack(cv_p), jnp.stack(cv_s))
```

```python
import functools
import math

import jax
import jax.numpy as jnp
from jax import lax
from jax.experimental import pallas as pl
from jax.experimental.pallas import tpu as pltpu

F32 = jnp.float32
BF16 = jnp.bfloat16

CHUNK = 64
BAND_PAST = 512
REL_CLIP = 128
ROT_DIM = 32
ROT_HALF = ROT_DIM // 2
ROPE_THETA = 500000.0
GATE_RANK = 16
GATE_NORM = 16.0
CONV_W = 3
EPS = 1e-6

LANE = 128
V7X_VMEM_LIMIT_BYTES = 56 * 1024 * 1024
MASKED = -1e30
MATMUL_ROWS = 2048
MATMUL_COLS = 512
MATMUL_PANEL_BYTES = 32 * 1024 * 1024
FFN_COLS = 256
BAND_Q_ROWS = 1024
ROPE_ROW_CHUNK = 512
FFN_ROW_CHUNK = 256

_NT = (((1,), (1,)), ((), ()))
_TN = (((0,), (0,)), ((), ()))


def _params(*semantics):
    return pltpu.CompilerParams(dimension_semantics=semantics, vmem_limit_bytes=V7X_VMEM_LIMIT_BYTES)


def _silu(x):
    return x / (1.0 + jnp.exp(-x))


def _rmsnorm_body(x_ref, w_ref, o_ref):
    x = x_ref[...]
    inv = lax.rsqrt(jnp.mean(x * x, axis=-1, keepdims=True) + EPS)
    o_ref[...] = (x * inv * w_ref[...]).astype(o_ref.dtype)


def _rmsnorm(x, w, out_dtype):
    m, d = x.shape
    tm = min(m, 256)
    return pl.pallas_call(
        _rmsnorm_body,
        grid=(m // tm,),
        in_specs=[pl.BlockSpec((tm, d), lambda i: (i, 0)), pl.BlockSpec((1, d), lambda i: (0, 0))],
        out_specs=pl.BlockSpec((tm, d), lambda i: (i, 0)),
        out_shape=jax.ShapeDtypeStruct((m, d), out_dtype),
        compiler_params=_params("parallel"),
        name="rmsnorm",
    )(x, w.reshape(1, d))


def _matmul_body(*refs, rope, resid, w_rows_are_outputs):
    a_ref, w_ref = refs[:2]
    rest = list(refs[2:])
    w = w_ref[...].astype(BF16)

    def product(a):
        if w_rows_are_outputs:
            return lax.dot_general(a, w, _NT, preferred_element_type=F32)
        return jnp.dot(a, w, preferred_element_type=F32)

    if not rope:
        acc = product(a_ref[...])
        if resid:
            acc = acc + rest.pop(0)[...]
        o_ref = rest[0]
        o_ref[...] = acc.astype(o_ref.dtype)
        return

    assert not resid
    cos_ref, sin_up_ref, sin_dn_ref, o_ref = rest
    tm = a_ref.shape[0]
    rc = min(ROPE_ROW_CHUNK, tm)
    for r0 in range(0, tm, rc):
        rows = slice(r0, r0 + rc)
        acc = product(a_ref[rows, :])
        cos, sin_up, sin_dn = cos_ref[rows, :], sin_up_ref[rows, :], sin_dn_ref[rows, :]
        for g in range(acc.shape[1] // LANE):
            x = acc[:, g * LANE:(g + 1) * LANE]
            y = x * cos + pltpu.roll(x, ROT_HALF, axis=1) * sin_up + pltpu.roll(x, LANE - ROT_HALF, axis=1) * sin_dn
            o_ref[rows, g * LANE:(g + 1) * LANE] = y.astype(o_ref.dtype)


def _matmul(a, w, layer, out_dtype, *, n=None, w_col=0, tk=None, a_kb=0, w_kb=0, resid=None, rope=None,
            w_rows_are_outputs=False):
    m = a.shape[0]
    tk = a.shape[1] if tk is None else tk
    tm = min(m, MATMUL_ROWS)
    cols = MATMUL_COLS
    if tm * tk * a.dtype.itemsize + 2 * tk * cols * 4 > MATMUL_PANEL_BYTES:
        cols //= 2
    if w_rows_are_outputs:
        tn = min(cols, n)
        assert m % tm == 0 and n % tn == 0 and w_col % tn == 0 and tk == a.shape[1] == w.shape[1]
        w_rb = w_col // tn
        w_spec = pl.BlockSpec((tn, tk), lambda i, j: (j + w_rb, 0))
    else:
        n = w.shape[2] if n is None else n
        tn = min(cols, n)
        assert m % tm == 0 and n % tn == 0 and w_col % tn == 0 and a.shape[1] % tk == 0 and w.shape[1] % tk == 0
        w_cb = w_col // tn
        w_spec = pl.BlockSpec((None, tk, tn), lambda i, j: (layer, w_kb, j + w_cb))
    in_specs = [pl.BlockSpec((tm, tk), lambda i, j: (i, a_kb), pipeline_mode=pl.Buffered(1)), w_spec]
    args = [a, w]
    if resid is not None:
        in_specs.append(pl.BlockSpec((tm, tn), lambda i, j: (i, j)))
        args.append(resid)
    if rope is not None:
        period = rope[0].shape[0] // tm
        for t in rope:
            in_specs.append(pl.BlockSpec((tm, LANE), lambda i, j: (i % period, 0)))
            args.append(t)
    return pl.pallas_call(
        functools.partial(_matmul_body, rope=rope is not None, resid=resid is not None,
                          w_rows_are_outputs=w_rows_are_outputs),
        grid=(m // tm, n // tn),
        in_specs=in_specs,
        out_specs=pl.BlockSpec((tm, tn), lambda i, j: (i, j)),
        out_shape=jax.ShapeDtypeStruct((m, n), out_dtype),
        compiler_params=_params("parallel", "arbitrary"),
        name="matmul",
    )(*args)


def _rope_tables(pos, rows):
    inv_freq = ROPE_THETA ** (-jnp.arange(ROT_HALF, dtype=F32) / ROT_HALF)
    ang = pos.astype(F32)[:, None] * inv_freq[None, :]
    cos, sin = jnp.cos(ang), jnp.sin(ang)
    t = pos.shape[0]
    pad = jnp.zeros((t, LANE - ROT_DIM), F32)
    zero = jnp.zeros((t, ROT_HALF), F32)
    c = jnp.concatenate([cos, cos, pad + 1.0], axis=1)
    s_up = jnp.concatenate([zero, sin, pad], axis=1)
    s_dn = jnp.concatenate([-sin, zero, pad], axis=1)
    rep = rows // t
    return tuple(jnp.tile(x, (rep, 1)) for x in (c, s_up, s_dn))


def _gla_body(*refs, nc, has_s0, heads, dk, dv):
    if has_s0:
        q_ref, k_ref, v_ref, g_ref, r_ref, wup_ref, b_ref, nw_ref, s0_ref, o_ref, sout_ref, st_ref = refs
    else:
        q_ref, k_ref, v_ref, g_ref, r_ref, wup_ref, b_ref, nw_ref, o_ref, sout_ref, st_ref = refs
    c = pl.program_id(1)
    L = q_ref.shape[0]
    SUB = 16

    @pl.when(c == 0)
    def _():
        for h in range(heads):
            if has_s0:
                st_ref[h] = s0_ref[0, h].T
            else:
                st_ref[h] = jnp.zeros(st_ref.shape[1:], F32)

    z = jnp.dot(r_ref[...], wup_ref[...], preferred_element_type=F32, precision=lax.Precision.HIGHEST) + b_ref[...]
    log_a = (jnp.minimum(z, 0.0) - jnp.log1p(jnp.exp(-jnp.abs(z)))) * (1.0 / GATE_NORM)
    row = lax.broadcasted_iota(jnp.int32, (L, 1), 0)
    cum_all = log_a
    s = 1
    while s < L:
        cum_all = cum_all + jnp.where(row >= s, pltpu.roll(cum_all, s, axis=0), 0.0)
        s *= 2
    sub_row = lax.broadcasted_iota(jnp.int32, (SUB, 1), 0)

    col = lax.broadcasted_iota(jnp.int32, (SUB, L), 1)
    heads_q, heads_k, heads_cum, heads_v, o_inters, att_offs = [], [], [], [], [], []
    for h in range(heads):
        q = q_ref[:, h * dk:(h + 1) * dk].astype(F32) * (dk ** -0.5)
        k = k_ref[:, h * dk:(h + 1) * dk].astype(F32)
        v = v_ref[:, h * dv:(h + 1) * dv]
        cum = cum_all[:, h * dk:(h + 1) * dk]
        st = st_ref[h]
        q_dec = (q * jnp.exp(cum)).astype(BF16)
        o_inters.append(lax.dot_general(q_dec, st.astype(BF16), _NT, preferred_element_type=F32))
        offs = [jnp.zeros((SUB, L), F32)]
        for blk in range(1, L // SUB):
            r0 = blk * SUB
            c_edge = cum[r0 - 1:r0]
            q_rel = (q[r0:r0 + SUB] * jnp.exp(cum[r0:r0 + SUB] - c_edge)).astype(BF16)
            k_rel = (k * jnp.exp(jnp.where(row < r0, c_edge - cum, -jnp.inf))).astype(BF16)
            offs.append(lax.dot_general(q_rel, k_rel, _NT, preferred_element_type=F32))
        c_last = cum[L - 1:L]
        k_dec = (k * jnp.exp(c_last - cum)).astype(BF16)
        st_ref[h] = st * jnp.exp(c_last) + lax.dot_general(v, k_dec, _TN, preferred_element_type=F32)
        heads_q.append(q)
        heads_k.append(k)
        heads_cum.append(cum)
        heads_v.append(v)
        att_offs.append(offs)

    atts = []
    for h in range(heads):
        q, k, cum = heads_q[h], heads_k[h], heads_cum[h]
        rows = []
        for blk in range(L // SUB):
            r0 = blk * SUB
            qb, kb, cb = q[r0:r0 + SUB], k[r0:r0 + SUB], cum[r0:r0 + SUB]
            att = att_offs[h][blk]
            for j in range(SUB):
                d = cb - cb[j:j + 1]
                e = jnp.exp(jnp.where(sub_row >= j, d, -jnp.inf))
                a = jnp.sum(qb * e * kb[j:j + 1], axis=1, keepdims=True)
                att = jnp.where(col == r0 + j, a, att)
            rows.append(att)
        atts.append(jnp.concatenate(rows, axis=0).astype(BF16))

    outs = [o_inters[h] + jnp.dot(atts[h], heads_v[h], preferred_element_type=F32) for h in range(heads)]
    for h in range(heads):
        o = outs[h]
        inv = lax.rsqrt(jnp.mean(o * o, axis=-1, keepdims=True) + EPS)
        gate = _silu(g_ref[:, h * dv:(h + 1) * dv].astype(F32))
        o_ref[:, h * dv:(h + 1) * dv] = (o * inv * nw_ref[...] * gate).astype(o_ref.dtype)

    @pl.when(c == nc - 1)
    def _():
        for h in range(heads):
            sout_ref[0, h] = st_ref[h].T


def _gla(pa, r, wup, b_alpha, norm_w, s0, *, batch, heads, dk, dv):
    m = pa.shape[0]
    t = m // batch
    L = min(CHUNK, t)
    nc = t // L
    wk, wv = heads * dk, heads * dv
    assert wv == 2 * wk, "column blocks below assume [q|k] together as wide as v and as g"
    row = lambda b, c: b * nc + c
    in_specs = [
        pl.BlockSpec((L, wk), lambda b, c: (row(b, c), 0)),
        pl.BlockSpec((L, wk), lambda b, c: (row(b, c), 1)),
        pl.BlockSpec((L, wv), lambda b, c: (row(b, c), 1)),
        pl.BlockSpec((L, wv), lambda b, c: (row(b, c), 2)),
        pl.BlockSpec((L, LANE), lambda b, c: (row(b, c), 0)),
        pl.BlockSpec((LANE, wk), lambda b, c: (0, 0)),
        pl.BlockSpec((1, wk), lambda b, c: (0, 0)),
        pl.BlockSpec((1, dv), lambda b, c: (0, 0)),
    ]
    args = [pa, pa, pa, pa, r, wup, b_alpha.reshape(1, -1), norm_w.reshape(1, dv)]
    if s0 is not None:
        in_specs.append(pl.BlockSpec((1, heads, dk, dv), lambda b, c: (b, 0, 0, 0)))
        args.append(s0)
    return pl.pallas_call(
        functools.partial(_gla_body, nc=nc, has_s0=s0 is not None, heads=heads, dk=dk, dv=dv),
        grid=(batch, nc),
        in_specs=in_specs,
        out_specs=[pl.BlockSpec((L, wv), lambda b, c: (row(b, c), 0)),
                   pl.BlockSpec((1, heads, dk, dv), lambda b, c: (b, 0, 0, 0))],
        out_shape=[jax.ShapeDtypeStruct((m, wv), BF16),
                   jax.ShapeDtypeStruct((batch, heads, dk, dv), F32)],
        scratch_shapes=[pltpu.VMEM((heads, dv, dk), F32)],
        compiler_params=_params("parallel", "arbitrary"),
        name="gla",
    )(*args)


def _toeplitz(lookup_row, tq, keep):
    w = lookup_row.shape[-1]
    return pltpu.roll(jnp.broadcast_to(lookup_row, (tq, w)), 0, axis=1, stride=1, stride_axis=0)[:, :keep]


def _band_softmax_pv(q, kp, vp, kn, vn, bias_p, bias_n, scale):
    s1 = lax.dot_general(q, kp, _NT, preferred_element_type=F32) * scale + bias_p
    s2 = lax.dot_general(q, kn, _NT, preferred_element_type=F32) * scale + bias_n
    mx = jnp.maximum(jnp.max(s1, axis=1, keepdims=True), jnp.max(s2, axis=1, keepdims=True))
    e1 = jnp.exp(s1 - mx)
    e2 = jnp.exp(s2 - mx)
    den = jnp.sum(e1, axis=1, keepdims=True) + jnp.sum(e2, axis=1, keepdims=True)
    o = (jnp.dot(e1.astype(BF16), vp, preferred_element_type=F32)
         + jnp.dot(e2.astype(BF16), vn, preferred_element_type=F32))
    return o * (1.0 / den)


def _band_prompt_body(q_ref, kp_ref, vp_ref, kn_ref, vn_ref, g1_ref, g2_ref, o_ref, bias_ref, *, scale):
    tq = q_ref.shape[0]
    band = BAND_PAST + CHUNK
    j = pl.program_id(2)

    @pl.when(jnp.logical_and(pl.program_id(1) == 0, j == 0))
    def _():
        bias_ref[:, :BAND_PAST] = _toeplitz(g1_ref[0], CHUNK, BAND_PAST)
        bias_ref[:, BAND_PAST:] = _toeplitz(g2_ref[0], CHUNK, CHUNK)

    k_all = jnp.concatenate([kp_ref[...].astype(BF16), kn_ref[...].astype(BF16)], axis=0)
    v_all = jnp.concatenate([vp_ref[...].astype(BF16), vn_ref[...].astype(BF16)], axis=0)
    bias = bias_ref[...]
    col = lax.broadcasted_iota(jnp.int32, (1, band), 1)
    chunks = range(0, tq, CHUNK)
    scores = [lax.dot_general(q_ref[r0:r0 + CHUNK, :], k_all[r0:r0 + band], _NT, preferred_element_type=F32)
              for r0 in chunks]
    probs = []
    for r0, s in zip(chunks, scores):
        s = s * scale + bias
        if r0 < BAND_PAST:
            s = s + jnp.where(jnp.logical_and(j == 0, col < BAND_PAST - r0), MASKED, 0.0)
        e = jnp.exp(s - jnp.max(s, axis=1, keepdims=True))
        probs.append((e.astype(BF16), 1.0 / jnp.sum(e, axis=1, keepdims=True)))
    outs = [jnp.dot(e, v_all[r0:r0 + band], preferred_element_type=F32) for r0, (e, _) in zip(chunks, probs)]
    for r0, o, (_, inv) in zip(chunks, outs, probs):
        o_ref[r0:r0 + CHUNK, :] = (o * inv).astype(o_ref.dtype)


def _band_sample_body(q_ref, kn_ref, vn_ref, g1_ref, g2_ref, kh_hbm, vh_hbm, o_ref,
                      kbuf, vbuf, bias_p_ref, bias_n_ref, sem, *, heads, hd, scale):
    b = pl.program_id(0)
    slot = b % 2
    tq = q_ref.shape[0]

    def copies(bb, sl):
        out = []
        for h in range(heads):
            out.append(pltpu.make_async_copy(kh_hbm.at[bb, :, h, :], kbuf.at[sl, h], sem.at[0, sl]))
            out.append(pltpu.make_async_copy(vh_hbm.at[bb, :, h, :], vbuf.at[sl, h], sem.at[1, sl]))
        return out

    @pl.when(b == 0)
    def _():
        for cp in copies(0, 0):
            cp.start()

        def build(h, carry):
            bias_p_ref[h] = _toeplitz(g1_ref[h], tq, BAND_PAST)
            bias_n_ref[h] = _toeplitz(g2_ref[h], tq, tq)
            return carry
        lax.fori_loop(0, heads, build, 0)

    @pl.when(b + 1 < pl.num_programs(0))
    def _():
        for cp in copies(b + 1, 1 - slot):
            cp.start()

    for cp in copies(b, slot):
        cp.wait()

    def head(h, carry):
        cols = pl.ds(pl.multiple_of(h * hd, hd), hd)
        o = _band_softmax_pv(q_ref[:, cols], kbuf[slot, h].astype(BF16), vbuf[slot, h].astype(BF16),
                             kn_ref[:, cols].astype(BF16), vn_ref[:, cols].astype(BF16),
                             bias_p_ref[h], bias_n_ref[h], scale)
        o_ref[:, cols] = o.astype(o_ref.dtype)
        return carry
    lax.fori_loop(0, heads, head, 0)


def _band_lookup_rows(table, tq):
    def f(d):
        return table[:, jnp.clip(d, -REL_CLIP, REL_CLIP) + REL_CLIP]
    n1 = jnp.arange(2 * BAND_PAST)
    g1 = f(jnp.where(n1 <= BAND_PAST, BAND_PAST - n1, REL_CLIP))
    w2 = max(2 * tq, LANE)
    n2 = jnp.arange(w2)
    g2 = f(jnp.where(n2 < w2 // 2, -n2, w2 - n2))
    return g1[:, None, :].astype(F32), g2[:, None, :].astype(F32)


def _band_prompt(qb, kb, vb, table, *, batch, heads, hd):
    m = qb.shape[0]
    t = m // batch
    tq = min(BAND_Q_ROWS, t)
    assert t % tq == 0 and tq % BAND_PAST == 0
    nb = t // tq
    per = tq // BAND_PAST
    g1, g2 = _band_lookup_rows(table, CHUNK)
    cur = lambda h, b, j: (b * nb + j, h)
    prev = lambda h, b, j: (jnp.maximum((b * nb + j) * per - 1, 0), h)
    return pl.pallas_call(
        functools.partial(_band_prompt_body, scale=hd ** -0.5),
        grid=(heads, batch, nb),
        in_specs=[pl.BlockSpec((tq, hd), cur),
                  pl.BlockSpec((BAND_PAST, hd), prev), pl.BlockSpec((BAND_PAST, hd), prev),
                  pl.BlockSpec((tq, hd), cur), pl.BlockSpec((tq, hd), cur),
                  pl.BlockSpec((1, 1, g1.shape[-1]), lambda h, b, j: (h, 0, 0)),
                  pl.BlockSpec((1, 1, g2.shape[-1]), lambda h, b, j: (h, 0, 0))],
        out_specs=pl.BlockSpec((tq, hd), cur),
        out_shape=jax.ShapeDtypeStruct((m, heads * hd), BF16),
        scratch_shapes=[pltpu.VMEM((CHUNK, BAND_PAST + CHUNK), F32)],
        compiler_params=_params("arbitrary", "arbitrary", "arbitrary"),
        name="band_prompt",
    )(qb, kb, vb, kb, vb, g1, g2)


def _band_sample(qb, kb, vb, k_hist, v_hist, table, *, batch, heads, hd):
    m = qb.shape[0]
    tq = m // batch
    keep = k_hist.shape[1]
    assert keep == BAND_PAST and tq == CHUNK
    g1, g2 = _band_lookup_rows(table, tq)
    cur = pl.BlockSpec((tq, heads * hd), lambda b: (b, 0))
    return pl.pallas_call(
        functools.partial(_band_sample_body, heads=heads, hd=hd, scale=hd ** -0.5),
        grid=(batch,),
        in_specs=[cur, cur, cur,
                  pl.BlockSpec(g1.shape, lambda b: (0, 0, 0)), pl.BlockSpec(g2.shape, lambda b: (0, 0, 0)),
                  pl.BlockSpec(memory_space=pl.ANY), pl.BlockSpec(memory_space=pl.ANY)],
        out_specs=cur,
        out_shape=jax.ShapeDtypeStruct((m, heads * hd), BF16),
        scratch_shapes=[pltpu.VMEM((2, heads, keep, hd), F32), pltpu.VMEM((2, heads, keep, hd), F32),
                        pltpu.VMEM((heads, tq, keep), F32), pltpu.VMEM((heads, tq, tq), F32),
                        pltpu.SemaphoreType.DMA((2, 2))],
        compiler_params=_params("arbitrary"),
        name="band_sample",
    )(qb, kb, vb, g1, g2, k_hist, v_hist)


def _lambda(lq1_ref, lk1_ref, lq2_ref, lk2_ref, lam_init):
    a = jnp.sum(lq1_ref[...] * lk1_ref[...], axis=1, keepdims=True)
    b = jnp.sum(lq2_ref[...] * lk2_ref[...], axis=1, keepdims=True)
    return jnp.exp(a) - jnp.exp(b) + lam_init


def _diff_finish(o, nw_ref, lam_init, o_ref):
    inv = lax.rsqrt(jnp.mean(o * o, axis=-1, keepdims=True) + EPS)
    o_ref[...] = (o * inv * nw_ref[...] * (1.0 - lam_init)).astype(o_ref.dtype)


def _diff_prompt_body(q_ref, k_ref, v_ref, lq1_ref, lk1_ref, lq2_ref, lk2_ref, nw_ref, o_ref,
                      k1_ref, k2_ref, vs_ref, *, nq, lam_init, scale):
    qi = pl.program_id(2)
    tq, hd2 = q_ref.shape
    hd = hd2 // 2

    @pl.when(qi == 0)
    def _():
        k = k_ref[...]
        k1_ref[...] = k[:, :hd].astype(BF16)
        k2_ref[...] = k[:, hd:].astype(BF16)
        vs_ref[...] = v_ref[...].astype(BF16)

    lam = _lambda(lq1_ref, lk1_ref, lq2_ref, lk2_ref, lam_init)
    q = q_ref[...]

    for n in range(nq):
        @pl.when(qi == n)
        def _(n=n):
            nk = tq * (n + 1)
            lo = nk - tq
            qc = lax.broadcasted_iota(jnp.int32, (tq, 1), 0) // CHUNK
            kc = lax.broadcasted_iota(jnp.int32, (1, tq), 1) // CHUNK
            visible = kc <= qc

            q1, q2 = q[:, :hd], q[:, hd:]
            own = [lax.dot_general(qh, ks[lo:nk, :], _NT, preferred_element_type=F32)
                   for qh, ks in ((q1, k1_ref), (q2, k2_ref))]
            past = [lax.dot_general(qh, ks[:lo, :], _NT, preferred_element_type=F32)
                    for qh, ks in ((q1, k1_ref), (q2, k2_ref))] if lo else [None, None]

            def weights(s_past, s_own):
                s_own = jnp.where(visible, s_own * scale, MASKED)
                mx = jnp.max(s_own, axis=1, keepdims=True)
                if s_past is not None:
                    s_past = s_past * scale
                    mx = jnp.maximum(mx, jnp.max(s_past, axis=1, keepdims=True))
                e_own = jnp.exp(s_own - mx)
                den = jnp.sum(e_own, axis=1, keepdims=True)
                e_past = None
                if s_past is not None:
                    e_past = jnp.exp(s_past - mx)
                    den = den + jnp.sum(e_past, axis=1, keepdims=True)
                return e_past, e_own, 1.0 / den

            def times_v(e_past, e_own):
                o = jnp.dot(e_own.astype(BF16), vs_ref[lo:nk, :], preferred_element_type=F32)
                if e_past is not None:
                    o = o + jnp.dot(e_past.astype(BF16), vs_ref[:lo, :], preferred_element_type=F32)
                return o

            ep1, eo1, r1 = weights(past[0], own[0])
            ep2, eo2, r2 = weights(past[1], own[1])
            o = times_v(ep1, eo1) * r1 - times_v(ep2, eo2) * (lam * r2)
            _diff_finish(o, nw_ref, lam_init, o_ref)


def _diff_prompt(q, k, v, lam_params, norm_w, *, batch, heads, hd, lam_init):
    m = q.shape[0]
    t = m // batch
    tq = 512
    nq = t // tq
    vec = pl.BlockSpec((1, hd), lambda b, h, i: (0, 0))
    return pl.pallas_call(
        functools.partial(_diff_prompt_body, nq=nq, lam_init=lam_init, scale=hd ** -0.5),
        grid=(batch, heads, nq),
        in_specs=[pl.BlockSpec((tq, 2 * hd), lambda b, h, i: (b * nq + i, h)),
                  pl.BlockSpec((t, 2 * hd), lambda b, h, i: (b, h)),
                  pl.BlockSpec((t, 2 * hd), lambda b, h, i: (b, h)),
                  vec, vec, vec, vec,
                  pl.BlockSpec((1, 2 * hd), lambda b, h, i: (0, 0))],
        out_specs=pl.BlockSpec((tq, 2 * hd), lambda b, h, i: (b * nq + i, h)),
        out_shape=jax.ShapeDtypeStruct((m, heads * 2 * hd), BF16),
        scratch_shapes=[pltpu.VMEM((t, hd), BF16), pltpu.VMEM((t, hd), BF16), pltpu.VMEM((t, 2 * hd), BF16)],
        compiler_params=_params("parallel", "parallel", "arbitrary"),
        name="diff_prompt",
    )(q, k, v, *[p.reshape(1, hd) for p in lam_params], norm_w.reshape(1, 2 * hd))


def _head_copies(k_hbm, v_hbm, kbuf, vbuf, sem, b, h, slot):
    return (pltpu.make_async_copy(k_hbm.at[b, :, h, :], kbuf.at[slot], sem.at[0, slot]),
            pltpu.make_async_copy(v_hbm.at[b, :, h, :], vbuf.at[slot], sem.at[1, slot]))


def _diff_sample_body(q_ref, kn_ref, vn_ref, lq1_ref, lk1_ref, lq2_ref, lk2_ref, nw_ref, kc_hbm, vc_hbm, o_ref,
                      kbuf, vbuf, sem, *, lam_init, scale):
    b, h = pl.program_id(0), pl.program_id(1)
    nh = pl.num_programs(1)
    nsteps = pl.num_programs(0) * nh
    step = b * nh + h
    slot = step % 2

    @pl.when(step == 0)
    def _():
        for cp in _head_copies(kc_hbm, vc_hbm, kbuf, vbuf, sem, 0, 0, 0):
            cp.start()

    @pl.when(step + 1 < nsteps)
    def _():
        nxt = step + 1
        for cp in _head_copies(kc_hbm, vc_hbm, kbuf, vbuf, sem, nxt // nh, nxt % nh, 1 - slot):
            cp.start()

    for cp in _head_copies(kc_hbm, vc_hbm, kbuf, vbuf, sem, b, h, slot):
        cp.wait()

    hd = q_ref.shape[1] // 2
    lam = _lambda(lq1_ref, lk1_ref, lq2_ref, lk2_ref, lam_init)
    q = q_ref[...]
    kc = kbuf[slot].astype(BF16)
    kn = kn_ref[...].astype(BF16)

    def weights(lo):
        qh = q[:, lo:lo + hd]
        sa = lax.dot_general(qh, kc[:, lo:lo + hd], _NT, preferred_element_type=F32) * scale
        sb = lax.dot_general(qh, kn[:, lo:lo + hd], _NT, preferred_element_type=F32) * scale
        mx = jnp.maximum(jnp.max(sa, axis=1, keepdims=True), jnp.max(sb, axis=1, keepdims=True))
        ea, eb = jnp.exp(sa - mx), jnp.exp(sb - mx)
        return ea, eb, 1.0 / (jnp.sum(ea, axis=1, keepdims=True) + jnp.sum(eb, axis=1, keepdims=True))

    ea1, eb1, r1 = weights(0)
    ea2, eb2, r2 = weights(hd)
    r2 = lam * r2
    pa = (ea1 * r1 - ea2 * r2).astype(BF16)
    pb = (eb1 * r1 - eb2 * r2).astype(BF16)
    o = (jnp.dot(pa, vbuf[slot].astype(BF16), preferred_element_type=F32)
         + jnp.dot(pb, vn_ref[...].astype(BF16), preferred_element_type=F32))
    _diff_finish(o, nw_ref, lam_init, o_ref)


def _diff_sample(q, k, v, k_hist, v_hist, lam_params, norm_w, *, batch, heads, hd, lam_init):
    m = q.shape[0]
    t = m // batch
    past = k_hist.shape[1]
    width = heads * 2 * hd
    cur = lambda b, h: (b, h)
    vec = pl.BlockSpec((1, hd), lambda b, h: (0, 0))
    return pl.pallas_call(
        functools.partial(_diff_sample_body, lam_init=lam_init, scale=hd ** -0.5),
        grid=(batch, heads),
        in_specs=[pl.BlockSpec((t, 2 * hd), cur), pl.BlockSpec((t, 2 * hd), cur), pl.BlockSpec((t, 2 * hd), cur),
                  vec, vec, vec, vec,
                  pl.BlockSpec((1, 2 * hd), lambda b, h: (0, 0)),
                  pl.BlockSpec(memory_space=pl.ANY), pl.BlockSpec(memory_space=pl.ANY)],
        out_specs=pl.BlockSpec((t, 2 * hd), cur),
        out_shape=jax.ShapeDtypeStruct((m, width), BF16),
        scratch_shapes=[pltpu.VMEM((2, past, 2 * hd), F32), pltpu.VMEM((2, past, 2 * hd), F32),
                        pltpu.SemaphoreType.DMA((2, 2))],
        compiler_params=_params("arbitrary", "arbitrary"),
        name="diff_sample",
    )(q, k, v, *[p.reshape(1, hd) for p in lam_params], norm_w.reshape(1, 2 * hd), k_hist, v_hist)


def _ffn_up_body(*refs, t, has_hist):
    if has_hist:
        a_ref, wg_ref, wv_ref, cwg_ref, cwv_ref, cbg_ref, cbv_ref, hg_ref, hv_ref, act_ref, sg_ref, sv_ref = refs
    else:
        a_ref, wg_ref, wv_ref, cwg_ref, cwv_ref, cbg_ref, cbv_ref, act_ref, sg_ref, sv_ref = refs
        hg_ref = hv_ref = None
    tm = a_ref.shape[0]
    tn = wg_ref.shape[1]
    rc = min(FFN_ROW_CHUNK, tm)
    wg = wg_ref[...].astype(BF16)
    wv = wv_ref[...].astype(BF16)
    row = lax.broadcasted_iota(jnp.int32, (rc, 1), 0)
    starts = list(range(0, rc, t)) if t < rc else [0]
    zero_row = jnp.zeros((1, tn), F32)

    def conv(u, prev_u, r0, h_ref, cw_ref, cb_ref, s_ref):
        u1 = pltpu.roll(u, 1, axis=0)
        u2 = pltpu.roll(u, 2, axis=0)
        for s in starts:
            g = r0 + s
            if g % t == 0:
                hist = None if h_ref is None else h_ref[g // t]
                m2 = zero_row if hist is None else hist[0:1]
                m1 = zero_row if hist is None else hist[1:2]
            else:
                m2, m1 = prev_u[rc - 2:rc - 1], prev_u[rc - 1:rc]
            u1 = jnp.where(row == s, m1, u1)
            u2 = jnp.where(row == s, m2, jnp.where(row == s + 1, m1, u2))
        ends = [s + t for s in starts] if t < rc else ([rc] if (r0 + rc) % t == 0 else [])
        for e in ends:
            s_ref[(r0 + e) // t - 1] = u[e - (CONV_W - 1):e]
        cw = cw_ref[...]
        return cb_ref[...] + u2 * cw[0:1] + u1 * cw[1:2] + u * cw[2:3]

    prev_g = prev_v = None
    for c in range(tm // rc):
        r0 = c * rc
        a = a_ref[r0:r0 + rc, :]
        ug = jnp.dot(a, wg, preferred_element_type=F32)
        uv = jnp.dot(a, wv, preferred_element_type=F32)
        gate = conv(ug, prev_g, r0, hg_ref, cwg_ref, cbg_ref, sg_ref)
        val = conv(uv, prev_v, r0, hv_ref, cwv_ref, cbv_ref, sv_ref)
        act_ref[r0:r0 + rc, :] = (_silu(gate) * val).astype(act_ref.dtype)
        prev_g, prev_v = ug, uv


def _ffn_up(h, w_up, layer, conv_w, conv_b, hist, *, batch, d_ff):
    m, d = h.shape
    t = m // batch
    tm = min(m, MATMUL_ROWS)
    tn = FFN_COLS
    rc = min(FFN_ROW_CHUNK, tm)
    assert m % tm == 0 and d_ff % tn == 0 and tm % t == 0, "a row tile must hold whole sequences"
    assert t % rc == 0 or rc % t == 0
    nj = d_ff // tn
    seqs = tm // t
    cb = conv_b.reshape(1, -1)
    state = lambda lo: pl.BlockSpec((seqs, CONV_W - 1, tn), lambda i, j: (i, 0, j + lo))
    in_specs = [pl.BlockSpec((tm, d), lambda i, j: (i, 0), pipeline_mode=pl.Buffered(1)),
                pl.BlockSpec((None, d, tn), lambda i, j: (layer, 0, j)),
                pl.BlockSpec((None, d, tn), lambda i, j: (layer, 0, j + nj)),
                pl.BlockSpec((CONV_W, tn), lambda i, j: (0, j)),
                pl.BlockSpec((CONV_W, tn), lambda i, j: (0, j + nj)),
                pl.BlockSpec((1, tn), lambda i, j: (0, j)),
                pl.BlockSpec((1, tn), lambda i, j: (0, j + nj))]
    args = [h, w_up, w_up, conv_w, conv_w, cb, cb]
    if hist is not None:
        in_specs += [state(0), state(nj)]
        args += [hist, hist]
    return pl.pallas_call(
        functools.partial(_ffn_up_body, t=t, has_hist=hist is not None),
        grid=(m // tm, nj),
        in_specs=in_specs,
        out_specs=[pl.BlockSpec((tm, tn), lambda i, j: (i, j)), state(0), state(0)],
        out_shape=[jax.ShapeDtypeStruct((m, d_ff), BF16),
                   jax.ShapeDtypeStruct((batch, CONV_W - 1, d_ff), F32),
                   jax.ShapeDtypeStruct((batch, CONV_W - 1, d_ff), F32)],
        compiler_params=_params("parallel", "arbitrary"),
        name="ffn_up",
    )(*args)


def _ffn(x, hist, norm_w, layer, w_up, conv_w, conv_b, w_down, *, batch):
    d_ff = w_down.shape[1]
    h = _rmsnorm(x, norm_w, BF16)
    act, sg, sv = _ffn_up(h, w_up, layer, conv_w, conv_b, hist, batch=batch, d_ff=d_ff)
    tk = d_ff // 2
    x = _matmul(act, w_down, layer, F32, tk=tk, a_kb=0, w_kb=0, resid=x)
    x = _matmul(act, w_down, layer, F32, tk=tk, a_kb=1, w_kb=1, resid=x)
    return x, jnp.concatenate([sg, sv], axis=-1)


def _even_layer(x, hist, norm_w, w, *, batch):
    heads_a, dk, dv = w["heads_a"], w["dk"], w["dv"]
    heads_b, hd = w["heads_b"], w["hd"]
    h = _rmsnorm(x, norm_w, BF16)
    nb = heads_b * hd
    e = w["index"]
    na = w["na"]
    mm_a = functools.partial(_matmul, h, w["w_in_t"], 0, w_rows_are_outputs=True)
    mm_b = functools.partial(_matmul, h, w["w_band_t"], 0, w_rows_are_outputs=True)
    pa = mm_a(BF16, n=na)
    r = mm_a(F32, n=LANE, w_col=na)
    qb = mm_b(BF16, n=nb)
    kb = mm_b(F32, n=nb, w_col=nb)
    vb = mm_b(F32, n=nb, w_col=2 * nb)
    s0 = None if hist is None else hist[0]
    o_a, state = _gla(pa, r, w["w_alpha_up"], w["b_alpha"], w["gla_norm_w"], s0,
                      batch=batch, heads=heads_a, dk=dk, dv=dv)
    if hist is None:
        o_b = _band_prompt(qb, kb, vb, w["table"], batch=batch, heads=heads_b, hd=hd)
    else:
        o_b = _band_sample(qb, kb, vb, hist[1], hist[2], w["table"], batch=batch, heads=heads_b, hd=hd)
    wa = heads_a * dv
    x = _matmul(o_a, w["w_out"], e, F32, tk=wa, a_kb=0, w_kb=0, resid=x)
    x = _matmul(o_b, w["w_out"], e, F32, tk=heads_b * hd, a_kb=0, w_kb=wa // (heads_b * hd), resid=x)
    return x, state, kb, vb


def _odd_layer(x, pos, hist, norm_w, w, *, batch):
    heads, hd = w["heads"], w["hd"]
    m = x.shape[0]
    h = _rmsnorm(x, norm_w, BF16)
    tm = min(m, MATMUL_ROWS)
    rope = _rope_tables(pos, tm if tm % pos.shape[0] == 0 else pos.shape[0])
    wc = heads * 2 * hd
    o_idx = w["index"]
    q = _matmul(h, w["w_in"], o_idx, BF16, n=wc, rope=rope)
    k = _matmul(h, w["w_in"], o_idx, F32, n=wc, w_col=wc, rope=rope)
    v = _matmul(h, w["w_in"], o_idx, F32, n=wc, w_col=2 * wc)
    lam_params = (w["lq1"], w["lk1"], w["lq2"], w["lk2"])
    if hist is None:
        o = _diff_prompt(q, k, v, lam_params, w["norm_w"], batch=batch, heads=heads, hd=hd, lam_init=w["lam_init"])
    else:
        o = _diff_sample(q, k, v, hist[0], hist[1], lam_params, w["norm_w"],
                         batch=batch, heads=heads, hd=hd, lam_init=w["lam_init"])
    x = _matmul(o, w["w_out"], o_idx, F32, resid=x)
    return x, k, v


def kernel(x_prompt, x_sample, state_gla, cache_band_k, cache_band_v, cache_diff_k, cache_diff_v, state_ffn_conv,
           norm_mix_w, norm_ffn_w, final_norm_w, w_in_even, w_alpha_up, b_alpha, gla_norm_w, rel_bias_table,
           w_out_even, w_in_odd, lambda_q1, lambda_k1, lambda_q2, lambda_k2, diff_norm_w, w_out_odd,
           w_ffn_up, ffn_conv_w, ffn_conv_b, w_ffn_down):
    bp, tp, d = x_prompt.shape
    bs, ts, _ = x_sample.shape
    depth = norm_mix_w.shape[0]
    past_len = cache_diff_k.shape[2]
    heads_a, dk, dv = state_gla.shape[2:]
    heads_b, hd_b = cache_band_k.shape[3:]
    heads_c, hd_c2 = cache_diff_k.shape[3:]
    hd_c = hd_c2 // 2
    band_keep = min(BAND_PAST, tp)

    xp = x_prompt.reshape(bp * tp, d)
    xs = x_sample.reshape(bs * ts, d)
    pos_p = jnp.arange(tp)
    pos_s = past_len + jnp.arange(ts)

    gla_p, gla_s, bk_p, bv_p, bk_s, bv_s = [], [], [], [], [], []
    dk_p, dv_p, dk_s, dv_s, cv_p, cv_s = [], [], [], [], [], []
    for layer in range(depth):
        if layer % 2 == 0:
            e = layer // 2
            na = 2 * heads_a * dk + 2 * heads_a * dv
            w_in_t = jnp.swapaxes(w_in_even[e], 0, 1)
            w = dict(
                heads_a=heads_a, dk=dk, dv=dv, heads_b=heads_b, hd=hd_b,
                index=e, w_in_t=w_in_t, w_band_t=w_in_t[na + GATE_RANK:], na=na,
                w_alpha_up=jnp.pad(w_alpha_up[e], ((0, LANE - GATE_RANK), (0, 0))),
                b_alpha=b_alpha[e], gla_norm_w=gla_norm_w[e], table=rel_bias_table[e],
                w_out=w_out_even,
            )
            xp, s_p, kp, vp = _even_layer(xp, None, norm_mix_w[layer], w, batch=bp)
            xs, s_s, ks, vs = _even_layer(xs, (state_gla[e], cache_band_k[e], cache_band_v[e]),
                                          norm_mix_w[layer], w, batch=bs)
            gla_p.append(s_p)
            gla_s.append(s_s)
            bk_p.append(kp.reshape(bp, tp, heads_b, hd_b)[:, tp - band_keep:])
            bv_p.append(vp.reshape(bp, tp, heads_b, hd_b)[:, tp - band_keep:])
            bk_s.append(ks.reshape(bs, ts, heads_b, hd_b))
            bv_s.append(vs.reshape(bs, ts, heads_b, hd_b))
        else:
            o = layer // 2
            w = dict(
                heads=heads_c, hd=hd_c, lam_init=0.8 - 0.6 * math.exp(-0.3 * layer),
                index=o, w_in=w_in_odd,
                lq1=lambda_q1[o], lk1=lambda_k1[o], lq2=lambda_q2[o], lk2=lambda_k2[o],
                norm_w=diff_norm_w[o], w_out=w_out_odd,
            )
            xp, kp, vp = _odd_layer(xp, pos_p, None, norm_mix_w[layer], w, batch=bp)
            xs, ks, vs = _odd_layer(xs, pos_s, (cache_diff_k[o], cache_diff_v[o]), norm_mix_w[layer], w, batch=bs)
            dk_p.append(kp.reshape(bp, tp, heads_c, hd_c2))
            dv_p.append(vp.reshape(bp, tp, heads_c, hd_c2))
            dk_s.append(ks.reshape(bs, ts, heads_c, hd_c2))
            dv_s.append(vs.reshape(bs, ts, heads_c, hd_c2))
        xp, c_p = _ffn(xp, None, norm_ffn_w[layer], layer, w_ffn_up, ffn_conv_w[layer], ffn_conv_b[layer],
                       w_ffn_down, batch=bp)
        xs, c_s = _ffn(xs, state_ffn_conv[layer], norm_ffn_w[layer], layer, w_ffn_up, ffn_conv_w[layer],
                       ffn_conv_b[layer], w_ffn_down, batch=bs)
        cv_p.append(c_p)
        cv_s.append(c_s)

    y_prompt = _rmsnorm(xp, final_norm_w, F32).reshape(bp, tp, d)
    y_sample = _rmsnorm(xs, final_norm_w, F32).reshape(bs, ts, d)
    return (y_prompt, y_sample,
            jnp.stack(gla_p), jnp.stack(gla_s),
            jnp.stack(bk_p), jnp.stack(bv_p), jnp.stack(bk_s), jnp.stack(bv_s),
            jnp.stack(dk_p), jnp.stack(dv_p), jnp.stack(dk_s), jnp.stack(dv_s),
            jnp.stack(cv_p), jnp.stack(cv_s))
```
